```python
import jax, jax.numpy as jnp
from jax import lax
import numpy as np

D_MODEL = 2048
BATCH = 8
SEQ = 2048
DEPTH = 1

CHUNK = 64
EPS = 1e-6

N_HEADS_A = 16
HEAD_DIM_A = 128
V_DIM_A = 128
KV_LATENT = 256
N_HEADS_IDX = 16
IDX_DIM = 128
TOPK_KEYS_MAX = 256
Q_BLOCK = 128
ATTN_SCALE = HEAD_DIM_A ** -0.5
IDX_W_SCALE = (N_HEADS_IDX ** -0.5) * (IDX_DIM ** -0.5)

SG_CHUNK = 128
N_GROUPS_B = 8
WIDTH_B = 2048
GROUP_DIM_B = WIDTH_B // N_GROUPS_B

N_EXPERTS = 32
TOP_K = 4
D_FF = 2048
SWIGLU_ALPHA = 1.702
SWIGLU_LIMIT = 7.0
MOE_BLOCK = 128

W_QA = N_HEADS_A * HEAD_DIM_A
W_QIDX = N_HEADS_IDX * IDX_DIM
SPLIT_POINTS = [
    W_QA,
    W_QA + KV_LATENT,
    W_QA + KV_LATENT + W_QIDX,
    W_QA + KV_LATENT + W_QIDX + IDX_DIM,
    W_QA + KV_LATENT + W_QIDX + IDX_DIM + N_HEADS_IDX,
    W_QA + KV_LATENT + W_QIDX + IDX_DIM + N_HEADS_IDX + 2 * WIDTH_B,
]
N_IN = SPLIT_POINTS[-1] + 2 * D_MODEL

kernel_name = "hybrid_dsa_gmlp_moe_adaln_block"


def rms_norm(x, g):
    xf = x.astype(jnp.float32)
    y = xf * lax.rsqrt(jnp.mean(xf * xf, axis=-1, keepdims=True) + EPS)
    return (y * g.astype(jnp.float32)).astype(x.dtype)


def layer_norm(x, g):
    xf = x.astype(jnp.float32)
    mu = jnp.mean(xf, axis=-1, keepdims=True)
    var = jnp.mean(jnp.square(xf - mu), axis=-1, keepdims=True)
    return ((xf - mu) * lax.rsqrt(var + EPS) * g.astype(jnp.float32)).astype(x.dtype)


def dsa_mla_attention(q_lat, c_kv, q_idx, k_idx, w_idx):
    B, S, H, C = q_lat.shape
    k_sel = min(TOPK_KEYS_MAX, S // 4)
    n_blk = S // Q_BLOCK
    key_chunk = jnp.arange(S) // CHUNK

    def to_blocks(a):
        return jnp.moveaxis(a.reshape((B, n_blk, Q_BLOCK) + a.shape[2:]), 1, 0)

    def one_block(args):
        blk, ql, qi, wi = args
        q_chunk = (blk * Q_BLOCK + jnp.arange(Q_BLOCK)) // CHUNK
        allowed = key_chunk[None, :] <= q_chunk[:, None]
        logits = jnp.einsum('bqhd,bsd->bqhs', qi, k_idx)
        idx_score = jnp.einsum('bqhs,bqh->bqs', jax.nn.relu(logits), wi).astype(jnp.float32)
        idx_score = jnp.where(allowed[None], idx_score, -jnp.inf)
        _, sel = lax.top_k(idx_score, k_sel)
        valid = key_chunk[sel] <= q_chunk[None, :, None]
        c_sel = jax.vmap(lambda c, i: c[i])(c_kv, sel)
        s = jnp.einsum('bqhc,bqkc->bqhk', ql, c_sel).astype(jnp.float32) * ATTN_SCALE
        s = jnp.where(valid[:, :, None, :], s, -jnp.inf)
        p = jax.nn.softmax(s, axis=-1).astype(c_sel.dtype)
        return jnp.einsum('bqhk,bqkc->bqhc', p, c_sel)

    o = lax.map(one_block, (jnp.arange(n_blk), to_blocks(q_lat), to_blocks(q_idx), to_blocks(w_idx)))
    return jnp.moveaxis(o, 0, 1).reshape(B, S, H, C)


def spatial_gating(z, ln_g, w_s, b_s):
    u, v = jnp.split(z, 2, axis=-1)
    v = layer_norm(v, ln_g)
    B, S, _ = v.shape
    v = v.reshape(B, S // SG_CHUNK, SG_CHUNK, N_GROUPS_B, GROUP_DIM_B)
    causal = jnp.tril(jnp.ones((SG_CHUNK, SG_CHUNK), dtype=w_s.dtype))
    w = w_s * causal[None]
    s = jnp.einsum('gts,bnsgd->bntgd', w, v) + jnp.transpose(b_s)[:, :, None]
    return u * s.reshape(B, S, WIDTH_B)


def clamped_swiglu(h):
    g = jnp.minimum(h[:, :D_FF], SWIGLU_LIMIT)
    lin = jnp.clip(h[:, D_FF:], -SWIGLU_LIMIT, SWIGLU_LIMIT)
    return g * jax.nn.sigmoid(SWIGLU_ALPHA * g) * (lin + 1.0)


def moe_ffn(xn, w_router, b_router, w_gate_up, b_gate_up, w_down, b_down):
    B, S, D = xn.shape
    T = B * S
    xf = xn.reshape(T, D)
    logits = (xf @ w_router + b_router).astype(jnp.float32)
    top_vals, top_idx = lax.top_k(logits, TOP_K)
    gates = jax.nn.softmax(top_vals, axis=-1).astype(xn.dtype)

    TK = T * TOP_K
    P = TK + N_EXPERTS * MOE_BLOCK
    n_blk = P // MOE_BLOCK
    flat_e = top_idx.reshape(-1)
    flat_tok = jnp.repeat(jnp.arange(T, dtype=jnp.int32), TOP_K)
    flat_g = gates.reshape(-1)
    order = jnp.argsort(flat_e)
    sorted_e = flat_e[order]
    counts = jnp.bincount(flat_e, length=N_EXPERTS)
    padded = ((counts + MOE_BLOCK - 1) // MOE_BLOCK) * MOE_BLOCK
    starts = jnp.cumsum(counts) - counts
    pad_ends = jnp.cumsum(padded)
    pstarts = pad_ends - padded
    dest = pstarts[sorted_e] + jnp.arange(TK) - starts[sorted_e]
    row_tok = jnp.zeros((P,), jnp.int32).at[dest].set(flat_tok[order])
    row_gate = jnp.zeros((P,), xn.dtype).at[dest].set(flat_g[order])
    block_e = jnp.clip(jnp.searchsorted(pad_ends, jnp.arange(n_blk) * MOE_BLOCK, side='right'),
                       0, N_EXPERTS - 1)

    def one_block(args):
        tok, g, e = args
        xb = xf[tok]
        h = xb @ w_gate_up[e] + b_gate_up[e]
        y = clamped_swiglu(h) @ w_down[e] + b_down[e]
        return y * g[:, None]

    ys = lax.map(one_block, (row_tok.reshape(n_blk, MOE_BLOCK), row_gate.reshape(n_blk, MOE_BLOCK), block_e))
    out = jnp.zeros((T, D), xn.dtype).at[row_tok].add(ys.reshape(P, D))
    return out.reshape(B, S, D)


def setup_inputs(seed: int = 0) -> dict:
    key = jax.random.key(seed)
    ks = jax.random.split(key, 32)
    f32 = jnp.float32

    def nrm(k, shape, scale):
        return jax.random.normal(k, shape, f32) * scale

    D, L = D_MODEL, DEPTH
    return {
        "x": nrm(ks[0], (BATCH, SEQ, D), 1.0),
        "c": nrm(ks[1], (BATCH, D), 1.0),
        "w_mod": nrm(ks[2], (L, D, 6 * D), 0.1 * D ** -0.5),
        "b_mod": nrm(ks[3], (L, 6 * D), 0.01),
        "norm1_g": 1.0 + nrm(ks[4], (L, D), 0.1),
        "w_in": nrm(ks[5], (L, D, N_IN), D ** -0.5),
        "kv_norm_g": 1.0 + nrm(ks[6], (L, KV_LATENT), 0.1),
        "w_uk": nrm(ks[7], (L, N_HEADS_A, HEAD_DIM_A, KV_LATENT), HEAD_DIM_A ** -0.5),
        "w_uv": nrm(ks[8], (L, N_HEADS_A, KV_LATENT, V_DIM_A), KV_LATENT ** -0.5),
        "w_proj_a": nrm(ks[9], (L, N_HEADS_A * V_DIM_A, D), (N_HEADS_A * V_DIM_A) ** -0.5),
        "sg_norm_g": 1.0 + nrm(ks[10], (L, WIDTH_B), 0.1),
        "w_spatial": nrm(ks[11], (L, N_GROUPS_B, SG_CHUNK, SG_CHUNK), SG_CHUNK ** -0.5),
        "b_spatial": nrm(ks[12], (L, N_GROUPS_B, SG_CHUNK), 0.02),
        "w_proj_b": nrm(ks[13], (L, WIDTH_B, D), WIDTH_B ** -0.5),
        "w_out": nrm(ks[14], (L, D, D), D ** -0.5),
        "norm2_g": 1.0 + nrm(ks[15], (L, D), 0.1),
        "w_router": nrm(ks[16], (L, D, N_EXPERTS), D ** -0.5),
        "b_router": nrm(ks[17], (L, N_EXPERTS), 0.01),
        "w_gate_up": nrm(ks[18], (L, N_EXPERTS, D, 2 * D_FF), D ** -0.5),
        "b_gate_up": nrm(ks[19], (L, N_EXPERTS, 2 * D_FF), 0.01),
        "w_down": nrm(ks[20], (L, N_EXPERTS, D_FF, D), D_FF ** -0.5),
        "b_down": nrm(ks[21], (L, N_EXPERTS, D), 0.01),
        "final_g": 1.0 + nrm(ks[22], (D,), 0.1),
    }


def reference(x, c, w_mod, b_mod, norm1_g, w_in, kv_norm_g, w_uk, w_uv, w_proj_a,
              sg_norm_g, w_spatial, b_spatial, w_proj_b, w_out, norm2_g,
              w_router, b_router, w_gate_up, b_gate_up, w_down, b_down, final_g):
    B, S, D = x.shape
    c_act = jax.nn.silu(c)
    for l in range(DEPTH):
        mod = c_act @ w_mod[l] + b_mod[l]
        sh1, sc1, g1, sh2, sc2, g2 = [m[:, None, :] for m in jnp.split(mod, 6, axis=-1)]

        h = rms_norm(x, norm1_g[l]) * (1.0 + sc1) + sh1
        proj = h @ w_in[l]
        q_a, c_kv, q_idx, k_idx, w_idx, z_b, gate_pre = jnp.split(proj, SPLIT_POINTS, axis=-1)

        q_a = q_a.reshape(B, S, N_HEADS_A, HEAD_DIM_A)
        c_kv = rms_norm(c_kv, kv_norm_g[l])
        q_lat = jnp.einsum('bshd,hdc->bshc', q_a, w_uk[l])
        q_idx = q_idx.reshape(B, S, N_HEADS_IDX, IDX_DIM)
        o_lat = dsa_mla_attention(q_lat, c_kv, q_idx, k_idx, w_idx * IDX_W_SCALE)
        o_a = jnp.einsum('bshc,hcv->bshv', o_lat, w_uv[l]).reshape(B, S, N_HEADS_A * V_DIM_A)
        y_a = o_a @ w_proj_a[l]

        sg = spatial_gating(jax.nn.gelu(z_b, approximate=False), sg_norm_g[l], w_spatial[l], b_spatial[l])
        y_b = sg @ w_proj_b[l]

        gate_a, gate_b = jnp.split(jax.nn.sigmoid(gate_pre), 2, axis=-1)
        mix = (gate_a * y_a + gate_b * y_b) @ w_out[l]
        x = x + g1 * mix

        h2 = rms_norm(x, norm2_g[l]) * (1.0 + sc2) + sh2
        x = x + g2 * moe_ffn(h2, w_router[l], b_router[l], w_gate_up[l], b_gate_up[l], w_down[l], b_down[l])
    return rms_norm(x, final_g)
```

```python
import functools

import jax
import jax.numpy as jnp
from jax import lax
from jax.experimental import pallas as pl
from jax.experimental.pallas import tpu as pltpu

F32 = jnp.float32
BF16 = jnp.bfloat16
I32 = jnp.int32

EPS = 1e-6
CHUNK = 64
N_HEADS = 16
HEAD_DIM = 128
KV_LATENT = 256
IDX_DIM = 128
TOPK_KEYS_MAX = 256
Q_BLOCK = 128
KEY_TILE = 256
N_GROUPS_B = 8
SG_CHUNK = 128
N_EXPERTS = 32
TOP_K = 4
SWIGLU_ALPHA = 1.702
SWIGLU_LIMIT = 7.0
ATTN_SCALE = HEAD_DIM ** -0.5
IDX_W_SCALE = (N_HEADS ** -0.5) * (IDX_DIM ** -0.5)

LANE = 128
MOE_ROWS = 512
MOE_FF_TILE = 256
NEG_BIAS = -1e30
INT_MIN = -2147483648
VMEM_LIMIT = 56 * 1024 * 1024


def _params(*sem):
    return pltpu.CompilerParams(dimension_semantics=sem, vmem_limit_bytes=VMEM_LIMIT)


def _mod_kernel(c_ref, w_ref, b_ref, o_ref):
    c = c_ref[...]
    ca = (c * jax.nn.sigmoid(c)).astype(BF16)
    o_ref[...] = jnp.dot(ca, w_ref[...].astype(BF16), preferred_element_type=F32) + b_ref[...]


def _modulation(c, w_mod, b_mod):
    bsz, d = c.shape
    n = w_mod.shape[1]
    tn = 1024
    return pl.pallas_call(
        _mod_kernel,
        grid=(n // tn,),
        in_specs=[pl.BlockSpec((bsz, d), lambda j: (0, 0)),
                  pl.BlockSpec((d, tn), lambda j: (0, j)),
                  pl.BlockSpec((1, tn), lambda j: (0, j))],
        out_specs=pl.BlockSpec((bsz, tn), lambda j: (0, j)),
        out_shape=jax.ShapeDtypeStruct((bsz, n), F32),
        compiler_params=_params("arbitrary"),
        name="modulation",
    )(c, w_mod, b_mod.reshape(1, n))


def _norm_mod_kernel(x_ref, g_ref, mod_ref, o_ref):
    x = x_ref[...]
    y = x * lax.rsqrt(jnp.mean(x * x, axis=-1, keepdims=True) + EPS) * g_ref[...]
    o_ref[...] = (y * (1.0 + mod_ref[0, 1:2, :]) + mod_ref[0, 0:1, :]).astype(o_ref.dtype)


def _norm_mod(x2, g, mod3, seq):
    t, d = x2.shape
    tm = 512
    return pl.pallas_call(
        _norm_mod_kernel,
        grid=(t // tm,),
        in_specs=[pl.BlockSpec((tm, d), lambda i: (i, 0)),
                  pl.BlockSpec((1, d), lambda i: (0, 0)),
                  pl.BlockSpec((1, 3, d), lambda i: ((i * tm) // seq, 0, 0))],
        out_specs=pl.BlockSpec((tm, d), lambda i: (i, 0)),
        out_shape=jax.ShapeDtypeStruct((t, d), BF16),
        compiler_params=_params("arbitrary"),
        name="norm_mod",
    )(x2, g.reshape(1, d), mod3)


def _mm_kernel(a_ref, w_ref, o_ref, *, act):
    acc = jnp.dot(a_ref[...], w_ref[...], preferred_element_type=F32)
    if act == "sigmoid":
        acc = jax.nn.sigmoid(acc)
    o_ref[...] = acc.astype(o_ref.dtype)


def _matmul(a, w, *, act=None, out_dtype=BF16, tm=1024, tn=1024, name="matmul"):
    m, k = a.shape
    n = w.shape[1]
    return pl.pallas_call(
        functools.partial(_mm_kernel, act=act),
        grid=(n // tn, m // tm),
        in_specs=[pl.BlockSpec((tm, k), lambda j, i: (i, 0)),
                  pl.BlockSpec((k, tn), lambda j, i: (0, j))],
        out_specs=pl.BlockSpec((tm, tn), lambda j, i: (i, j)),
        out_shape=jax.ShapeDtypeStruct((m, n), out_dtype),
        compiler_params=_params("arbitrary", "arbitrary"),
        name=name,
    )(a, w)


def _small_proj_kernel(a_ref, w_ref, g_ref, ckv_ref, kidx_ref, widx_ref):
    acc = jnp.dot(a_ref[...], w_ref[...], preferred_element_type=F32)
    c = acc[:, :KV_LATENT]
    cn = c * lax.rsqrt(jnp.mean(c * c, axis=-1, keepdims=True) + EPS) * g_ref[...]
    ckv_ref[...] = cn.astype(ckv_ref.dtype)
    kidx_ref[...] = acc[:, KV_LATENT:KV_LATENT + IDX_DIM].astype(kidx_ref.dtype)
    widx_ref[...] = acc[:, KV_LATENT + IDX_DIM:] * IDX_W_SCALE


def _small_proj(h, w_small, kv_g):
    t, d = h.shape
    n = w_small.shape[1]
    tm = 1024
    return pl.pallas_call(
        _small_proj_kernel,
        grid=(t // tm,),
        in_specs=[pl.BlockSpec((tm, d), lambda i: (i, 0)),
                  pl.BlockSpec((d, n), lambda i: (0, 0)),
                  pl.BlockSpec((1, KV_LATENT), lambda i: (0, 0))],
        out_specs=[pl.BlockSpec((tm, KV_LATENT), lambda i: (i, 0)),
                   pl.BlockSpec((tm, IDX_DIM), lambda i: (i, 0)),
                   pl.BlockSpec((tm, LANE), lambda i: (i, 0))],
        out_shape=[jax.ShapeDtypeStruct((t, KV_LATENT), BF16),
                   jax.ShapeDtypeStruct((t, IDX_DIM), BF16),
                   jax.ShapeDtypeStruct((t, LANE), F32)],
        compiler_params=_params("arbitrary"),
        name="small_proj",
    )(h, w_small, kv_g.reshape(1, KV_LATENT))


def _indexer_kernel(q_ref, k_ref, w_ref, bias_ref, key_scr, *, n_kt, k_sel):
    i = pl.program_id(1)
    q = q_ref[...]
    qs = jnp.concatenate([q[:, h * IDX_DIM:(h + 1) * IDX_DIM] for h in range(N_HEADS)], axis=0)
    w = w_ref[...]
    wcols = [w[:, h:h + 1] for h in range(N_HEADS)]
    row = lax.broadcasted_iota(I32, (Q_BLOCK, KEY_TILE), 0)
    col = lax.broadcasted_iota(I32, (Q_BLOCK, KEY_TILE), 1)
    q_chunk = (i * Q_BLOCK + row) // CHUNK
    q_end = (i + 1) * Q_BLOCK

    for j in range(n_kt):
        @pl.when(j * KEY_TILE < q_end)
        def _():
            k = k_ref[0, j * KEY_TILE:(j + 1) * KEY_TILE, :]
            logits = lax.dot_general(qs, k, (((1,), (1,)), ((), ())),
                                     preferred_element_type=F32)
            sc = wcols[0] * jnp.maximum(logits[0:Q_BLOCK], 0.0)
            for h in range(1, N_HEADS):
                sc = sc + wcols[h] * jnp.maximum(logits[h * Q_BLOCK:(h + 1) * Q_BLOCK], 0.0)
            bits = lax.bitcast_convert_type(sc, I32)
            key = bits ^ ((bits >> 31) & 0x7FFFFFFF)
            allowed = (j * KEY_TILE + col) // CHUNK <= q_chunk
            key_scr[j] = jnp.where(allowed, key, INT_MIN)

        @pl.when(j * KEY_TILE >= q_end)
        def _():
            key_scr[j] = jnp.full((Q_BLOCK, KEY_TILE), INT_MIN, I32)

    def bisect(it, t_u):
        cand_u = t_u | jnp.left_shift(jnp.int32(1), 31 - it)
        cand_s = cand_u ^ INT_MIN
        cnt = jnp.zeros((Q_BLOCK, KEY_TILE), F32)
        for j in range(n_kt):
            cnt = cnt + jnp.where(key_scr[j] >= cand_s, 1.0, 0.0)
        tot = jnp.sum(cnt, axis=-1, keepdims=True)
        return jnp.where(tot >= float(k_sel), cand_u, t_u)

    t_u = lax.fori_loop(0, 32, bisect, jnp.zeros((Q_BLOCK, 1), I32))
    t_s = jnp.maximum(t_u ^ INT_MIN, INT_MIN + 1)
    for j in range(n_kt):
        bias_ref[0, 0, j] = jnp.where(key_scr[j] >= t_s, 0.0, NEG_BIAS).astype(bias_ref.dtype)


def _indexer(q_idx, k_idx3, w_idx, bsz, seq):
    n_q = seq // Q_BLOCK
    n_kt = seq // KEY_TILE
    k_sel = min(TOPK_KEYS_MAX, seq // 4)
    return pl.pallas_call(
        functools.partial(_indexer_kernel, n_kt=n_kt, k_sel=k_sel),
        grid=(bsz, n_q),
        in_specs=[pl.BlockSpec((Q_BLOCK, N_HEADS * IDX_DIM), lambda b, i: (b * n_q + i, 0)),
                  pl.BlockSpec((1, seq, IDX_DIM), lambda b, i: (b, 0, 0)),
                  pl.BlockSpec((Q_BLOCK, LANE), lambda b, i: (b * n_q + i, 0))],
        out_specs=pl.BlockSpec((1, 1, n_kt, Q_BLOCK, KEY_TILE), lambda b, i: (b, i, 0, 0, 0)),
        out_shape=jax.ShapeDtypeStruct((bsz, n_q, n_kt, Q_BLOCK, KEY_TILE), BF16),
        scratch_shapes=[pltpu.VMEM((n_kt, Q_BLOCK, KEY_TILE), I32)],
        compiler_params=_params("arbitrary", "arbitrary"),
        name="indexer",
    )(q_idx, k_idx3, w_idx)


def _attn_kernel(qa_ref, c_ref, bias_ref, wuk_ref, wuv_ref, o_ref,
                 s_scr, q_scr, m_scr, l_scr, acc_scr):
    i = pl.program_id(1)
    n_used = (i * Q_BLOCK) // KEY_TILE + 1
    rows = N_HEADS * Q_BLOCK

    qa = qa_ref[...]
    for h in range(N_HEADS):
        ql = jnp.dot(qa[:, h * HEAD_DIM:(h + 1) * HEAD_DIM], wuk_ref[h],
                     preferred_element_type=F32)
        q_scr[h * Q_BLOCK:(h + 1) * Q_BLOCK, :] = (ql * ATTN_SCALE).astype(BF16)

    m_scr[...] = jnp.full((rows, LANE), -jnp.inf, F32)

    def scores(j, carry):
        off = pl.multiple_of(j * KEY_TILE, KEY_TILE)
        k = c_ref[0, pl.ds(off, KEY_TILE), :]
        s = lax.dot_general(q_scr[...], k, (((1,), (1,)), ((), ())),
                            preferred_element_type=F32)
        b = bias_ref[0, 0, j].astype(F32)
        for h in range(N_HEADS):
            sl = slice(h * Q_BLOCK, (h + 1) * Q_BLOCK)
            sb = s[sl] + b
            s_scr[j, sl, :] = sb
            m_scr[sl, :] = jnp.maximum(m_scr[sl, :], jnp.maximum(sb[:, :LANE], sb[:, LANE:]))
        return carry

    lax.fori_loop(0, n_used, scores, 0)

    m = jnp.max(m_scr[...], axis=-1, keepdims=True)
    l_scr[...] = jnp.zeros((rows, LANE), F32)
    acc_scr[...] = jnp.zeros((rows, KV_LATENT), F32)

    def values(j, carry):
        off = pl.multiple_of(j * KEY_TILE, KEY_TILE)
        p = jnp.exp(s_scr[j] - m)
        l_scr[...] += p[:, :LANE] + p[:, LANE:]
        acc_scr[...] += jnp.dot(p.astype(BF16), c_ref[0, pl.ds(off, KEY_TILE), :],
                                preferred_element_type=F32)
        return carry

    lax.fori_loop(0, n_used, values, 0)

    inv_l = 1.0 / jnp.sum(l_scr[...], axis=-1, keepdims=True)
    o = (acc_scr[...] * inv_l).astype(BF16)
    for h in range(N_HEADS):
        o_ref[:, h * HEAD_DIM:(h + 1) * HEAD_DIM] = jnp.dot(
            o[h * Q_BLOCK:(h + 1) * Q_BLOCK], wuv_ref[h],
            preferred_element_type=F32).astype(o_ref.dtype)


def _attention(q_a, c_kv3, bias, w_uk, w_uv, bsz, seq):
    n_q = seq // Q_BLOCK
    n_kt = seq // KEY_TILE
    rows = N_HEADS * Q_BLOCK
    t = bsz * seq
    return pl.pallas_call(
        _attn_kernel,
        grid=(bsz, n_q),
        in_specs=[pl.BlockSpec((Q_BLOCK, N_HEADS * HEAD_DIM), lambda b, i: (b * n_q + i, 0)),
                  pl.BlockSpec((1, seq, KV_LATENT), lambda b, i: (b, 0, 0)),
                  pl.BlockSpec((1, 1, n_kt, Q_BLOCK, KEY_TILE), lambda b, i: (b, i, 0, 0, 0)),
                  pl.BlockSpec((N_HEADS, HEAD_DIM, KV_LATENT), lambda b, i: (0, 0, 0)),
                  pl.BlockSpec((N_HEADS, KV_LATENT, HEAD_DIM), lambda b, i: (0, 0, 0))],
        out_specs=pl.BlockSpec((Q_BLOCK, N_HEADS * HEAD_DIM), lambda b, i: (b * n_q + i, 0)),
        out_shape=jax.ShapeDtypeStruct((t, N_HEADS * HEAD_DIM), BF16),
        scratch_shapes=[pltpu.VMEM((n_kt, rows, KEY_TILE), F32),
                        pltpu.VMEM((rows, KV_LATENT), BF16),
                        pltpu.VMEM((rows, LANE), F32),
                        pltpu.VMEM((rows, LANE), F32),
                        pltpu.VMEM((rows, KV_LATENT), F32)],
        compiler_params=_params("arbitrary", "arbitrary"),
        name="attention",
    )(q_a, c_kv3, bias, w_uk, w_uv)


def _spatial_kernel(z_ref, g_ref, ws_ref, bs_ref, o_ref, *, width):
    z = z_ref[...].astype(F32)
    z = 0.5 * z * (1.0 + lax.erf(z * (2.0 ** -0.5)))
    u = z[:, :width]
    v = z[:, width:]
    mu = jnp.mean(v, axis=-1, keepdims=True)
    vc = v - mu
    var = jnp.mean(vc * vc, axis=-1, keepdims=True)
    vn = (vc * lax.rsqrt(var + EPS) * g_ref[...]).astype(BF16)
    r = lax.broadcasted_iota(I32, (SG_CHUNK, SG_CHUNK), 0)
    c = lax.broadcasted_iota(I32, (SG_CHUNK, SG_CHUNK), 1)
    gd = width // N_GROUPS_B
    bs = bs_ref[...]
    for g in range(N_GROUPS_B):
        wg = jnp.where(r >= c, ws_ref[g], 0.0).astype(BF16)
        s = jnp.dot(wg, vn[:, g * gd:(g + 1) * gd], preferred_element_type=F32) + bs[:, g:g + 1]
        o_ref[:, g * gd:(g + 1) * gd] = (u[:, g * gd:(g + 1) * gd] * s).astype(o_ref.dtype)


def _spatial_gating(z, sg_g, w_spatial, b_spatial_t):
    t, w2 = z.shape
    width = w2 // 2
    return pl.pallas_call(
        functools.partial(_spatial_kernel, width=width),
        grid=(t // SG_CHUNK,),
        in_specs=[pl.BlockSpec((SG_CHUNK, w2), lambda i: (i, 0)),
                  pl.BlockSpec((1, width), lambda i: (0, 0)),
                  pl.BlockSpec((N_GROUPS_B, SG_CHUNK, SG_CHUNK), lambda i: (0, 0, 0)),
                  pl.BlockSpec((SG_CHUNK, LANE), lambda i: (0, 0))],
        out_specs=pl.BlockSpec((SG_CHUNK, width), lambda i: (i, 0)),
        out_shape=jax.ShapeDtypeStruct((t, width), BF16),
        compiler_params=_params("arbitrary"),
        name="spatial_gating",
    )(z, sg_g.reshape(1, width), w_spatial, b_spatial_t)


def _mix_kernel(a_ref, b_ref, wa_ref, wb_ref, ga_ref, gb_ref, o_ref):
    ya = jnp.dot(a_ref[...], wa_ref[...], preferred_element_type=F32)
    yb = jnp.dot(b_ref[...], wb_ref[...], preferred_element_type=F32)
    o_ref[...] = (ga_ref[...].astype(F32) * ya + gb_ref[...].astype(F32) * yb).astype(o_ref.dtype)


def _mix(o_a, sg, w_a, w_b, gates):
    t, k = o_a.shape
    d = w_a.shape[1]
    tm, tn = 1024, 1024
    nb = d // tn
    return pl.pallas_call(
        _mix_kernel,
        grid=(nb, t // tm),
        in_specs=[pl.BlockSpec((tm, k), lambda j, i: (i, 0)),
                  pl.BlockSpec((tm, k), lambda j, i: (i, 0)),
                  pl.BlockSpec((k, tn), lambda j, i: (0, j)),
                  pl.BlockSpec((k, tn), lambda j, i: (0, j)),
                  pl.BlockSpec((tm, tn), lambda j, i: (i, j)),
                  pl.BlockSpec((tm, tn), lambda j, i: (i, nb + j))],
        out_specs=pl.BlockSpec((tm, tn), lambda j, i: (i, j)),
        out_shape=jax.ShapeDtypeStruct((t, d), BF16),
        compiler_params=_params("arbitrary", "arbitrary"),
        name="branch_mix",
    )(o_a, sg, w_a, w_b, gates, gates)


def _out_router_kernel(a_ref, w_ref, x_ref, mod_ref, g_ref, wr_ref, br_ref,
                       x1_ref, h2_ref, lg_ref):
    mix = jnp.dot(a_ref[...], w_ref[...], preferred_element_type=F32)
    x1 = x_ref[...] + mod_ref[0, 2:3, :] * mix
    x1_ref[...] = x1
    y = x1 * lax.rsqrt(jnp.mean(x1 * x1, axis=-1, keepdims=True) + EPS) * g_ref[...]
    h2 = y * (1.0 + mod_ref[0, 4:5, :]) + mod_ref[0, 3:4, :]
    h2_ref[...] = h2
    h_hi = h2.astype(BF16)
    h_lo = (h2 - h_hi.astype(F32)).astype(BF16)
    wr = wr_ref[...]
    w_hi = wr.astype(BF16)
    w_lo = (wr - w_hi.astype(F32)).astype(BF16)
    lg = jnp.dot(h_hi, w_hi, preferred_element_type=F32)
    lg = lg + jnp.dot(h_lo, w_hi, preferred_element_type=F32)
    lg = lg + jnp.dot(h_hi, w_lo, preferred_element_type=F32)
    lg_ref[...] = lg + br_ref[...]


def _out_router(mixpre, w_out, x2, mod6, norm2_g, w_router_p, b_router_p, seq):
    t, d = x2.shape
    tm = 256
    return pl.pallas_call(
        _out_router_kernel,
        grid=(t // tm,),
        in_specs=[pl.BlockSpec((tm, d), lambda i: (i, 0)),
                  pl.BlockSpec((d, d), lambda i: (0, 0)),
                  pl.BlockSpec((tm, d), lambda i: (i, 0)),
                  pl.BlockSpec((1, 6, d), lambda i: ((i * tm) // seq, 0, 0)),
                  pl.BlockSpec((1, d), lambda i: (0, 0)),
                  pl.BlockSpec((d, LANE), lambda i: (0, 0)),
                  pl.BlockSpec((1, LANE), lambda i: (0, 0))],
        out_specs=[pl.BlockSpec((tm, d), lambda i: (i, 0)),
                   pl.BlockSpec((tm, d), lambda i: (i, 0)),
                   pl.BlockSpec((tm, LANE), lambda i: (i, 0))],
        out_shape=[jax.ShapeDtypeStruct((t, d), F32),
                   jax.ShapeDtypeStruct((t, d), F32),
                   jax.ShapeDtypeStruct((t, LANE), F32)],
        compiler_params=_params("arbitrary"),
        name="out_proj_router",
    )(mixpre, w_out, x2, mod6, norm2_g.reshape(1, d), w_router_p, b_router_p)


DISPATCH_TOKENS = 512


def _dispatch_kernel(pos_ref, h_hbm, xs_init_hbm, xs_hbm, sem):
    del xs_init_hbm
    i = pl.program_id(0)

    def row_copy(t, k):
        return pltpu.make_async_copy(
            h_hbm.at[pl.ds(i * DISPATCH_TOKENS + t, 1), :],
            xs_hbm.at[pl.ds(pos_ref[0, 0, t * TOP_K + k], 1), :],
            sem)

    def start(t, carry):
        for k in range(TOP_K):
            row_copy(t, k).start()
        return carry

    def wait(t, carry):
        for k in range(TOP_K):
            row_copy(t, k).wait()
        return carry

    lax.fori_loop(0, DISPATCH_TOKENS, start, 0)
    lax.fori_loop(0, DISPATCH_TOKENS, wait, 0)


def _dispatch(pos, h2, n_rows):
    t, d = h2.shape
    n_tiles = t // DISPATCH_TOKENS
    pos3 = pos.reshape(n_tiles, 1, DISPATCH_TOKENS * TOP_K)
    xs0 = jnp.zeros((n_rows, d), F32)
    return pl.pallas_call(
        _dispatch_kernel,
        grid=(n_tiles,),
        in_specs=[pl.BlockSpec((1, 1, DISPATCH_TOKENS * TOP_K), lambda i: (i, 0, 0),
                               memory_space=pltpu.SMEM),
                  pl.BlockSpec(memory_space=pl.ANY),
                  pl.BlockSpec(memory_space=pl.ANY)],
        out_specs=pl.BlockSpec(memory_space=pl.ANY),
        out_shape=jax.ShapeDtypeStruct((n_rows, d), F32),
        scratch_shapes=[pltpu.SemaphoreType.DMA(())],
        input_output_aliases={2: 0},
        compiler_params=_params("arbitrary"),
        name="moe_dispatch",
    )(pos3, h2, xs0)


def _ffn_kernel(be_ref, nu_ref, x_ref, wg_ref, wl_ref, bg_ref, bl_ref, wd_ref, bd_ref,
                o_ref, xb_scr):
    i = pl.program_id(0)
    f = pl.program_id(1)
    used = i < nu_ref[0]

    @pl.when(f == 0)
    def _():
        xb_scr[...] = x_ref[...].astype(BF16)
        o_ref[...] = jnp.broadcast_to(bd_ref[0], o_ref.shape)

    @pl.when(used)
    def _():
        xb = xb_scr[...]
        g = jnp.dot(xb, wg_ref[0].astype(BF16), preferred_element_type=F32) + bg_ref[0]
        lin = jnp.dot(xb, wl_ref[0].astype(BF16), preferred_element_type=F32) + bl_ref[0]
        g = jnp.minimum(g, SWIGLU_LIMIT)
        lin = jnp.clip(lin, -SWIGLU_LIMIT, SWIGLU_LIMIT)
        act = g * jax.nn.sigmoid(SWIGLU_ALPHA * g) * (lin + 1.0)
        o_ref[...] += jnp.dot(act.astype(BF16), wd_ref[0].astype(BF16),
                              preferred_element_type=F32)


def _expert_ffn(block_e, n_used, xs, w_gate_up, b_gate_up3, w_down, b_down3):
    p, d = xs.shape
    n_e, _, f2 = w_gate_up.shape
    d_ff = f2 // 2
    tf = MOE_FF_TILE
    nf = d_ff // tf
    n_blk = p // MOE_ROWS
    grid_spec = pltpu.PrefetchScalarGridSpec(
        num_scalar_prefetch=2,
        grid=(n_blk, nf),
        in_specs=[pl.BlockSpec((MOE_ROWS, d), lambda i, f, be, nu: (i, 0)),
                  pl.BlockSpec((1, d, tf), lambda i, f, be, nu: (be[i], 0, f)),
                  pl.BlockSpec((1, d, tf), lambda i, f, be, nu: (be[i], 0, nf + f)),
                  pl.BlockSpec((1, 1, tf), lambda i, f, be, nu: (be[i], 0, f)),
                  pl.BlockSpec((1, 1, tf), lambda i, f, be, nu: (be[i], 0, nf + f)),
                  pl.BlockSpec((1, tf, d), lambda i, f, be, nu: (be[i], f, 0)),
                  pl.BlockSpec((1, 1, d), lambda i, f, be, nu: (be[i], 0, 0))],
        out_specs=pl.BlockSpec((MOE_ROWS, d), lambda i, f, be, nu: (i, 0)),
        scratch_shapes=[pltpu.VMEM((MOE_ROWS, d), BF16)],
    )
    return pl.pallas_call(
        _ffn_kernel,
        grid_spec=grid_spec,
        out_shape=jax.ShapeDtypeStruct((p, d), F32),
        compiler_params=_params("arbitrary", "arbitrary"),
        name="expert_ffn",
    )(block_e, n_used, xs, w_gate_up, w_gate_up, b_gate_up3, b_gate_up3, w_down, b_down3)


COMBINE_TOKENS = 128


def _combine_kernel(pos_ref, ys_hbm, gate_ref, x1_ref, mod_ref, g_ref, o_ref, buf, sem):
    def row_copy(t, k):
        return pltpu.make_async_copy(
            ys_hbm.at[pl.ds(pos_ref[0, 0, t * TOP_K + k], 1), :],
            buf.at[k, pl.ds(t, 1), :],
            sem)

    def start(t, carry):
        for k in range(TOP_K):
            row_copy(t, k).start()
        return carry

    def wait(t, carry):
        for k in range(TOP_K):
            row_copy(t, k).wait()
        return carry

    lax.fori_loop(0, COMBINE_TOKENS, start, 0)
    lax.fori_loop(0, COMBINE_TOKENS, wait, 0)

    gate = gate_ref[...]
    moe = gate[:, 0:1] * buf[0]
    for k in range(1, TOP_K):
        moe = moe + gate[:, k:k + 1] * buf[k]
    x = x1_ref[...] + mod_ref[0, 5:6, :] * moe
    o_ref[...] = x * lax.rsqrt(jnp.mean(x * x, axis=-1, keepdims=True) + EPS) * g_ref[...]


def _combine(pos, ys, gates_p, x1, mod6, final_g, seq):
    t, d = x1.shape
    tt = COMBINE_TOKENS
    n_tiles = t // tt
    pos3 = pos.reshape(n_tiles, 1, tt * TOP_K)
    return pl.pallas_call(
        _combine_kernel,
        grid=(n_tiles,),
        in_specs=[pl.BlockSpec((1, 1, tt * TOP_K), lambda i: (i, 0, 0), memory_space=pltpu.SMEM),
                  pl.BlockSpec(memory_space=pl.ANY),
                  pl.BlockSpec((tt, LANE), lambda i: (i, 0)),
                  pl.BlockSpec((tt, d), lambda i: (i, 0)),
                  pl.BlockSpec((1, 6, d), lambda i: ((i * tt) // seq, 0, 0)),
                  pl.BlockSpec((1, d), lambda i: (0, 0))],
        out_specs=pl.BlockSpec((tt, d), lambda i: (i, 0)),
        out_shape=jax.ShapeDtypeStruct((t, d), F32),
        scratch_shapes=[pltpu.VMEM((TOP_K, tt, d), F32),
                        pltpu.SemaphoreType.DMA(())],
        compiler_params=_params("arbitrary"),
        name="moe_combine",
    )(pos3, ys, gates_p, x1, mod6, final_g.reshape(1, d))


def _routing(logits_p):
    t = logits_p.shape[0]
    logits = logits_p[:, :N_EXPERTS]
    top_vals, top_idx = lax.top_k(logits, TOP_K)
    gates = jax.nn.softmax(top_vals, axis=-1)
    onehot = jnp.sum((top_idx[:, :, None] == jnp.arange(N_EXPERTS, dtype=I32)).astype(I32), axis=1)
    rank = jnp.cumsum(onehot, axis=0) - onehot
    counts = jnp.sum(onehot, axis=0)
    padded = ((counts + MOE_ROWS - 1) // MOE_ROWS) * MOE_ROWS
    pad_ends = jnp.cumsum(padded)
    pstarts = pad_ends - padded
    pos = (pstarts[top_idx] + jnp.take_along_axis(rank, top_idx, axis=1)).astype(I32)
    n_rows = t * TOP_K + N_EXPERTS * MOE_ROWS
    n_blk = n_rows // MOE_ROWS
    block_e = jnp.clip(jnp.searchsorted(pad_ends, jnp.arange(n_blk, dtype=I32) * MOE_ROWS, side="right"),
                       0, N_EXPERTS - 1).astype(I32)
    n_used = (pad_ends[-1] // MOE_ROWS).astype(I32).reshape(1)
    gates_p = jnp.pad(gates, ((0, 0), (0, LANE - TOP_K)))
    return pos, gates_p, block_e, n_used, n_rows


def _layer(x2, c_act_in, bsz, seq, w_mod, b_mod, norm1_g, w_in, kv_norm_g, w_uk, w_uv,
           w_proj_a, sg_norm_g, w_spatial, b_spatial, w_proj_b, w_out, norm2_g,
           w_router, b_router, w_gate_up, b_gate_up, w_down, b_down):
    t, d = x2.shape
    w_qa = N_HEADS * HEAD_DIM
    w_qi = N_HEADS * IDX_DIM
    width_b = sg_norm_g.shape[0]

    mod = _modulation(c_act_in, w_mod, b_mod)
    mod6 = mod.reshape(bsz, 6, d)
    h = _norm_mod(x2, norm1_g, mod6[:, :3], seq)

    o0 = w_qa
    o1 = o0 + KV_LATENT
    o2 = o1 + w_qi
    o3 = o2 + IDX_DIM
    o4 = o3 + N_HEADS
    o5 = o4 + 2 * width_b
    w_small = jnp.concatenate(
        [w_in[:, o0:o1], w_in[:, o2:o3], w_in[:, o3:o4],
         jnp.zeros((d, LANE - N_HEADS), w_in.dtype)], axis=1).astype(BF16)

    q_a = _matmul(h, w_in[:, :o0].astype(BF16), name="proj_q_a")
    q_idx = _matmul(h, w_in[:, o1:o2].astype(BF16), name="proj_q_idx")
    z_b = _matmul(h, w_in[:, o4:o5].astype(BF16), name="proj_z_b")
    gates = _matmul(h, w_in[:, o5:].astype(BF16), act="sigmoid", name="proj_gates")
    c_kv, k_idx, w_idx = _small_proj(h, w_small, kv_norm_g)

    bias = _indexer(q_idx, k_idx.reshape(bsz, seq, IDX_DIM), w_idx, bsz, seq)
    o_a = _attention(q_a, c_kv.reshape(bsz, seq, KV_LATENT), bias,
                     w_uk.astype(BF16), w_uv.astype(BF16), bsz, seq)

    b_sp_t = jnp.pad(jnp.transpose(b_spatial), ((0, 0), (0, LANE - N_GROUPS_B)))
    sg = _spatial_gating(z_b, sg_norm_g, w_spatial, b_sp_t)

    mixpre = _mix(o_a, sg, w_proj_a.astype(BF16), w_proj_b.astype(BF16), gates)

    w_router_p = jnp.pad(w_router, ((0, 0), (0, LANE - N_EXPERTS)))
    b_router_p = jnp.pad(b_router, (0, LANE - N_EXPERTS)).reshape(1, LANE)
    x1, h2, logits_p = _out_router(mixpre, w_out.astype(BF16), x2, mod6, norm2_g,
                                   w_router_p, b_router_p, seq)

    pos, gates_p, block_e, n_used, n_rows = _routing(logits_p)
    xs = _dispatch(pos.reshape(-1), h2, n_rows)
    n_e = w_gate_up.shape[0]
    ys = _expert_ffn(block_e, n_used, xs, w_gate_up, b_gate_up.reshape(n_e, 1, -1),
                     w_down, b_down.reshape(n_e, 1, -1))
    return pos, ys, gates_p, x1, mod6


def kernel(x, c, w_mod, b_mod, norm1_g, w_in, kv_norm_g, w_uk, w_uv, w_proj_a, sg_norm_g,
           w_spatial, b_spatial, w_proj_b, w_out, norm2_g, w_router, b_router, w_gate_up,
           b_gate_up, w_down, b_down, final_g):
    bsz, seq, d = x.shape
    depth = w_mod.shape[0]
    assert depth == 1, "single-layer block"
    x2 = x.reshape(bsz * seq, d)
    pos, ys, gates_p, x1, mod6 = _layer(
        x2, c, bsz, seq, w_mod[0], b_mod[0], norm1_g[0], w_in[0], kv_norm_g[0], w_uk[0],
        w_uv[0], w_proj_a[0], sg_norm_g[0], w_spatial[0], b_spatial[0], w_proj_b[0],
        w_out[0], norm2_g[0], w_router[0], b_router[0], w_gate_up[0], b_gate_up[0],
        w_down[0], b_down[0])
    out = _combine(pos.reshape(-1), ys, gates_p, x1, mod6, final_g, seq)
    return out.reshape(bsz, seq, d)
```

```python
import functools

import jax
import jax.numpy as jnp
from jax import lax
from jax.experimental import pallas as pl
from jax.experimental.pallas import tpu as pltpu

F32 = jnp.float32
BF16 = jnp.bfloat16
I32 = jnp.int32

EPS = 1e-6
CHUNK = 64
N_HEADS = 16
HEAD_DIM = 128
KV_LATENT = 256
IDX_DIM = 128
TOPK_KEYS_MAX = 256
Q_BLOCK = 128
KEY_TILE = 256
N_GROUPS_B = 8
SG_CHUNK = 128
N_EXPERTS = 32
TOP_K = 4
SWIGLU_ALPHA = 1.702
SWIGLU_LIMIT = 7.0
ATTN_SCALE = HEAD_DIM ** -0.5
IDX_W_SCALE = (N_HEADS ** -0.5) * (IDX_DIM ** -0.5)

LANE = 128
MOE_ROWS = 512
ITEM_SUB = 2
MOE_FF_TILE = 256
NEG_BIAS = -1e30
INT_MIN = -2147483648
VMEM_LIMIT = 56 * 1024 * 1024


def _params(*sem):
    return pltpu.CompilerParams(dimension_semantics=sem, vmem_limit_bytes=VMEM_LIMIT)


def _mod_kernel(c_ref, w_ref, b_ref, o_ref):
    c = c_ref[...]
    ca = (c * jax.nn.sigmoid(c)).astype(BF16)
    o_ref[...] = jnp.dot(ca, w_ref[...].astype(BF16), preferred_element_type=F32) + b_ref[...]


def _modulation(c, w_mod, b_mod):
    bsz, d = c.shape
    n = w_mod.shape[1]
    tn = 1024
    return pl.pallas_call(
        _mod_kernel,
        grid=(n // tn,),
        in_specs=[pl.BlockSpec((bsz, d), lambda j: (0, 0)),
                  pl.BlockSpec((d, tn), lambda j: (0, j)),
                  pl.BlockSpec((1, tn), lambda j: (0, j))],
        out_specs=pl.BlockSpec((bsz, tn), lambda j: (0, j)),
        out_shape=jax.ShapeDtypeStruct((bsz, n), F32),
        compiler_params=_params("arbitrary"),
        name="modulation",
    )(c, w_mod, b_mod.reshape(1, n))


def _norm_mod_kernel(x_ref, g_ref, mod_ref, o_ref):
    x = x_ref[...]
    y = x * lax.rsqrt(jnp.mean(x * x, axis=-1, keepdims=True) + EPS) * g_ref[...]
    o_ref[...] = (y * (1.0 + mod_ref[0, 1:2, :]) + mod_ref[0, 0:1, :]).astype(o_ref.dtype)


def _norm_mod(x2, g, mod3, seq):
    t, d = x2.shape
    tm = 512
    return pl.pallas_call(
        _norm_mod_kernel,
        grid=(t // tm,),
        in_specs=[pl.BlockSpec((tm, d), lambda i: (i, 0)),
                  pl.BlockSpec((1, d), lambda i: (0, 0)),
                  pl.BlockSpec((1, 3, d), lambda i: ((i * tm) // seq, 0, 0))],
        out_specs=pl.BlockSpec((tm, d), lambda i: (i, 0)),
        out_shape=jax.ShapeDtypeStruct((t, d), BF16),
        compiler_params=_params("arbitrary"),
        name="norm_mod",
    )(x2, g.reshape(1, d), mod3)


def _mm_kernel(a_ref, w_ref, o_ref, *, act):
    acc = jnp.dot(a_ref[...], w_ref[...], preferred_element_type=F32)
    if act == "sigmoid":
        acc = jax.nn.sigmoid(acc)
    o_ref[...] = acc.astype(o_ref.dtype)


def _matmul(a, w, *, act=None, out_dtype=BF16, tm=1024, tn=1024, name="matmul"):
    m, k = a.shape
    n = w.shape[1]
    return pl.pallas_call(
        functools.partial(_mm_kernel, act=act),
        grid=(n // tn, m // tm),
        in_specs=[pl.BlockSpec((tm, k), lambda j, i: (i, 0)),
                  pl.BlockSpec((k, tn), lambda j, i: (0, j))],
        out_specs=pl.BlockSpec((tm, tn), lambda j, i: (i, j)),
        out_shape=jax.ShapeDtypeStruct((m, n), out_dtype),
        compiler_params=_params("arbitrary", "arbitrary"),
        name=name,
    )(a, w)


def _small_proj_kernel(a_ref, w_ref, g_ref, ckv_ref, kidx_ref, widx_ref):
    acc = jnp.dot(a_ref[...], w_ref[...], preferred_element_type=F32)
    c = acc[:, :KV_LATENT]
    cn = c * lax.rsqrt(jnp.mean(c * c, axis=-1, keepdims=True) + EPS) * g_ref[...]
    ckv_ref[...] = cn.astype(ckv_ref.dtype)
    kidx_ref[...] = acc[:, KV_LATENT:KV_LATENT + IDX_DIM].astype(kidx_ref.dtype)
    widx_ref[...] = acc[:, KV_LATENT + IDX_DIM:] * IDX_W_SCALE


def _small_proj(h, w_small, kv_g):
    t, d = h.shape
    n = w_small.shape[1]
    tm = 1024
    return pl.pallas_call(
        _small_proj_kernel,
        grid=(t // tm,),
        in_specs=[pl.BlockSpec((tm, d), lambda i: (i, 0)),
                  pl.BlockSpec((d, n), lambda i: (0, 0)),
                  pl.BlockSpec((1, KV_LATENT), lambda i: (0, 0))],
        out_specs=[pl.BlockSpec((tm, KV_LATENT), lambda i: (i, 0)),
                   pl.BlockSpec((tm, IDX_DIM), lambda i: (i, 0)),
                   pl.BlockSpec((tm, LANE), lambda i: (i, 0))],
        out_shape=[jax.ShapeDtypeStruct((t, KV_LATENT), BF16),
                   jax.ShapeDtypeStruct((t, IDX_DIM), BF16),
                   jax.ShapeDtypeStruct((t, LANE), F32)],
        compiler_params=_params("arbitrary"),
        name="small_proj",
    )(h, w_small, kv_g.reshape(1, KV_LATENT))


def _indexer_kernel(q_ref, k_ref, w_ref, bias_ref, key_scr, keyt_scr, *, n_kt, k_sel):
    i = pl.program_id(1)
    q = q_ref[...]
    qs = jnp.concatenate([q[:, h * IDX_DIM:(h + 1) * IDX_DIM] for h in range(N_HEADS)], axis=0)
    w = w_ref[...]
    wcols = [w[:, h:h + 1] for h in range(N_HEADS)]
    row = lax.broadcasted_iota(I32, (Q_BLOCK, KEY_TILE), 0)
    col = lax.broadcasted_iota(I32, (Q_BLOCK, KEY_TILE), 1)
    q_chunk = (i * Q_BLOCK + row) // CHUNK
    n_used = (i * Q_BLOCK) // KEY_TILE + 1

    for j in range(n_kt):
        @pl.when(j < n_used)
        def _():
            k = k_ref[0, j * KEY_TILE:(j + 1) * KEY_TILE, :]
            logits = lax.dot_general(qs, k, (((1,), (1,)), ((), ())),
                                     preferred_element_type=F32)
            sc = wcols[0] * jnp.maximum(logits[0:Q_BLOCK], 0.0)
            for h in range(1, N_HEADS):
                sc = sc + wcols[h] * jnp.maximum(logits[h * Q_BLOCK:(h + 1) * Q_BLOCK], 0.0)
            bits = lax.bitcast_convert_type(sc, I32)
            key = bits ^ ((bits >> 31) & 0x7FFFFFFF)
            allowed = (j * KEY_TILE + col) // CHUNK <= q_chunk
            key = jnp.where(allowed, key, INT_MIN)
            key_scr[j] = key
            keyt_scr[j] = key.T

    def bisect(it, t_u):
        cand_u = t_u | jnp.left_shift(jnp.int32(1), 31 - it)
        cand_s = cand_u ^ INT_MIN

        def count(j, cnt):
            return cnt + jnp.where(keyt_scr[j] >= cand_s, 1.0, 0.0)

        cnt = lax.fori_loop(0, n_used, count, jnp.zeros((KEY_TILE, Q_BLOCK), F32))
        tot = jnp.sum(cnt, axis=0, keepdims=True)
        return jnp.where(tot >= float(k_sel), cand_u, t_u)

    t_u = lax.fori_loop(0, 32, bisect, jnp.zeros((1, Q_BLOCK), I32))
    t_s = jnp.maximum(t_u ^ INT_MIN, INT_MIN + 1)
    t_col = jnp.broadcast_to(t_s, (Q_BLOCK, Q_BLOCK)).T
    t_tile = jnp.concatenate([t_col] * (KEY_TILE // Q_BLOCK), axis=1)
    for j in range(n_kt):
        @pl.when(j < n_used)
        def _():
            bias_ref[0, 0, j] = jnp.where(key_scr[j] >= t_tile, 0.0, NEG_BIAS).astype(bias_ref.dtype)

        @pl.when(j >= n_used)
        def _():
            bias_ref[0, 0, j] = jnp.full((Q_BLOCK, KEY_TILE), NEG_BIAS, bias_ref.dtype)


def _indexer(q_idx, k_idx3, w_idx, bsz, seq):
    n_q = seq // Q_BLOCK
    n_kt = seq // KEY_TILE
    k_sel = min(TOPK_KEYS_MAX, seq // 4)
    return pl.pallas_call(
        functools.partial(_indexer_kernel, n_kt=n_kt, k_sel=k_sel),
        grid=(bsz, n_q),
        in_specs=[pl.BlockSpec((Q_BLOCK, N_HEADS * IDX_DIM), lambda b, i: (b * n_q + i, 0)),
                  pl.BlockSpec((1, seq, IDX_DIM), lambda b, i: (b, 0, 0)),
                  pl.BlockSpec((Q_BLOCK, LANE), lambda b, i: (b * n_q + i, 0))],
        out_specs=pl.BlockSpec((1, 1, n_kt, Q_BLOCK, KEY_TILE), lambda b, i: (b, i, 0, 0, 0)),
        out_shape=jax.ShapeDtypeStruct((bsz, n_q, n_kt, Q_BLOCK, KEY_TILE), BF16),
        scratch_shapes=[pltpu.VMEM((n_kt, Q_BLOCK, KEY_TILE), I32),
                        pltpu.VMEM((n_kt, KEY_TILE, Q_BLOCK), I32)],
        compiler_params=_params("arbitrary", "arbitrary"),
        name="indexer",
    )(q_idx, k_idx3, w_idx)


def _attn_kernel(qa_ref, c_ref, bias_ref, wuk_ref, wuv_ref, o_ref,
                 s_scr, q_scr, m_scr, l_scr, acc_scr):
    i = pl.program_id(1)
    n_used = (i * Q_BLOCK) // KEY_TILE + 1
    rows = N_HEADS * Q_BLOCK

    qa = qa_ref[...]
    for h in range(N_HEADS):
        ql = jnp.dot(qa[:, h * HEAD_DIM:(h + 1) * HEAD_DIM], wuk_ref[h],
                     preferred_element_type=F32)
        q_scr[h * Q_BLOCK:(h + 1) * Q_BLOCK, :] = (ql * ATTN_SCALE).astype(BF16)

    m_scr[...] = jnp.full((rows, LANE), -jnp.inf, F32)

    def scores(j, carry):
        off = pl.multiple_of(j * KEY_TILE, KEY_TILE)
        k = c_ref[0, pl.ds(off, KEY_TILE), :]
        s = lax.dot_general(q_scr[...], k, (((1,), (1,)), ((), ())),
                            preferred_element_type=F32)
        b = bias_ref[0, 0, j].astype(F32)
        for h in range(N_HEADS):
            sl = slice(h * Q_BLOCK, (h + 1) * Q_BLOCK)
            sb = s[sl] + b
            s_scr[j, sl, :] = sb
            m_scr[sl, :] = jnp.maximum(m_scr[sl, :], jnp.maximum(sb[:, :LANE], sb[:, LANE:]))
        return carry

    lax.fori_loop(0, n_used, scores, 0)

    m = jnp.max(m_scr[...], axis=-1, keepdims=True)
    l_scr[...] = jnp.zeros((rows, LANE), F32)
    acc_scr[...] = jnp.zeros((rows, KV_LATENT), F32)

    def values(j, carry):
        off = pl.multiple_of(j * KEY_TILE, KEY_TILE)
        p = jnp.exp(s_scr[j] - m)
        l_scr[...] += p[:, :LANE] + p[:, LANE:]
        acc_scr[...] += jnp.dot(p.astype(BF16), c_ref[0, pl.ds(off, KEY_TILE), :],
                                preferred_element_type=F32)
        return carry

    lax.fori_loop(0, n_used, values, 0)

    inv_l = 1.0 / jnp.sum(l_scr[...], axis=-1, keepdims=True)
    o = (acc_scr[...] * inv_l).astype(BF16)
    for h in range(N_HEADS):
        o_ref[:, h * HEAD_DIM:(h + 1) * HEAD_DIM] = jnp.dot(
            o[h * Q_BLOCK:(h + 1) * Q_BLOCK], wuv_ref[h],
            preferred_element_type=F32).astype(o_ref.dtype)


def _attention(q_a, c_kv3, bias, w_uk, w_uv, bsz, seq):
    n_q = seq // Q_BLOCK
    n_kt = seq // KEY_TILE
    rows = N_HEADS * Q_BLOCK
    t = bsz * seq
    return pl.pallas_call(
        _attn_kernel,
        grid=(bsz, n_q),
        in_specs=[pl.BlockSpec((Q_BLOCK, N_HEADS * HEAD_DIM), lambda b, i: (b * n_q + i, 0)),
                  pl.BlockSpec((1, seq, KV_LATENT), lambda b, i: (b, 0, 0)),
                  pl.BlockSpec((1, 1, n_kt, Q_BLOCK, KEY_TILE), lambda b, i: (b, i, 0, 0, 0)),
                  pl.BlockSpec((N_HEADS, HEAD_DIM, KV_LATENT), lambda b, i: (0, 0, 0)),
                  pl.BlockSpec((N_HEADS, KV_LATENT, HEAD_DIM), lambda b, i: (0, 0, 0))],
        out_specs=pl.BlockSpec((Q_BLOCK, N_HEADS * HEAD_DIM), lambda b, i: (b * n_q + i, 0)),
        out_shape=jax.ShapeDtypeStruct((t, N_HEADS * HEAD_DIM), BF16),
        scratch_shapes=[pltpu.VMEM((n_kt, rows, KEY_TILE), F32),
                        pltpu.VMEM((rows, KV_LATENT), BF16),
                        pltpu.VMEM((rows, LANE), F32),
                        pltpu.VMEM((rows, LANE), F32),
                        pltpu.VMEM((rows, KV_LATENT), F32)],
        compiler_params=_params("arbitrary", "arbitrary"),
        name="attention",
    )(q_a, c_kv3, bias, w_uk, w_uv)


def _spatial_kernel(z_ref, g_ref, ws_ref, bs_ref, o_ref, *, width):
    z = z_ref[...].astype(F32)
    z = 0.5 * z * (1.0 + lax.erf(z * (2.0 ** -0.5)))
    u = z[:, :width]
    v = z[:, width:]
    mu = jnp.mean(v, axis=-1, keepdims=True)
    vc = v - mu
    var = jnp.mean(vc * vc, axis=-1, keepdims=True)
    vn = (vc * lax.rsqrt(var + EPS) * g_ref[...]).astype(BF16)
    r = lax.broadcasted_iota(I32, (SG_CHUNK, SG_CHUNK), 0)
    c = lax.broadcasted_iota(I32, (SG_CHUNK, SG_CHUNK), 1)
    gd = width // N_GROUPS_B
    bs = bs_ref[...]
    for g in range(N_GROUPS_B):
        wg = jnp.where(r >= c, ws_ref[g], 0.0).astype(BF16)
        s = jnp.dot(wg, vn[:, g * gd:(g + 1) * gd], preferred_element_type=F32) + bs[:, g:g + 1]
        o_ref[:, g * gd:(g + 1) * gd] = (u[:, g * gd:(g + 1) * gd] * s).astype(o_ref.dtype)


def _spatial_gating(z, sg_g, w_spatial, b_spatial_t):
    t, w2 = z.shape
    width = w2 // 2
    return pl.pallas_call(
        functools.partial(_spatial_kernel, width=width),
        grid=(t // SG_CHUNK,),
        in_specs=[pl.BlockSpec((SG_CHUNK, w2), lambda i: (i, 0)),
                  pl.BlockSpec((1, width), lambda i: (0, 0)),
                  pl.BlockSpec((N_GROUPS_B, SG_CHUNK, SG_CHUNK), lambda i: (0, 0, 0)),
                  pl.BlockSpec((SG_CHUNK, LANE), lambda i: (0, 0))],
        out_specs=pl.BlockSpec((SG_CHUNK, width), lambda i: (i, 0)),
        out_shape=jax.ShapeDtypeStruct((t, width), BF16),
        compiler_params=_params("arbitrary"),
        name="spatial_gating",
    )(z, sg_g.reshape(1, width), w_spatial, b_spatial_t)


def _mix_kernel(a_ref, b_ref, wa_ref, wb_ref, ga_ref, gb_ref, o_ref):
    ya = jnp.dot(a_ref[...], wa_ref[...], preferred_element_type=F32)
    yb = jnp.dot(b_ref[...], wb_ref[...], preferred_element_type=F32)
    o_ref[...] = (ga_ref[...].astype(F32) * ya + gb_ref[...].astype(F32) * yb).astype(o_ref.dtype)


def _mix(o_a, sg, w_a, w_b, gates):
    t, k = o_a.shape
    d = w_a.shape[1]
    tm, tn = 1024, 1024
    nb = d // tn
    return pl.pallas_call(
        _mix_kernel,
        grid=(nb, t // tm),
        in_specs=[pl.BlockSpec((tm, k), lambda j, i: (i, 0)),
                  pl.BlockSpec((tm, k), lambda j, i: (i, 0)),
                  pl.BlockSpec((k, tn), lambda j, i: (0, j)),
                  pl.BlockSpec((k, tn), lambda j, i: (0, j)),
                  pl.BlockSpec((tm, tn), lambda j, i: (i, j)),
                  pl.BlockSpec((tm, tn), lambda j, i: (i, nb + j))],
        out_specs=pl.BlockSpec((tm, tn), lambda j, i: (i, j)),
        out_shape=jax.ShapeDtypeStruct((t, d), BF16),
        compiler_params=_params("arbitrary", "arbitrary"),
        name="branch_mix",
    )(o_a, sg, w_a, w_b, gates, gates)


def _out_router_kernel(a_ref, w_ref, x_ref, mod_ref, g_ref, wr_ref, br_ref,
                       x1_ref, h2_ref, lg_ref):
    mix = jnp.dot(a_ref[...], w_ref[...], preferred_element_type=F32)
    x1 = x_ref[...] + mod_ref[0, 2:3, :] * mix
    x1_ref[...] = x1
    y = x1 * lax.rsqrt(jnp.mean(x1 * x1, axis=-1, keepdims=True) + EPS) * g_ref[...]
    h2 = y * (1.0 + mod_ref[0, 4:5, :]) + mod_ref[0, 3:4, :]
    h2_ref[...] = h2
    h_hi = h2.astype(BF16)
    h_lo = (h2 - h_hi.astype(F32)).astype(BF16)
    wr = wr_ref[...]
    w_hi = wr.astype(BF16)
    w_lo = (wr - w_hi.astype(F32)).astype(BF16)
    lg = jnp.dot(h_hi, w_hi, preferred_element_type=F32)
    lg = lg + jnp.dot(h_lo, w_hi, preferred_element_type=F32)
    lg = lg + jnp.dot(h_hi, w_lo, preferred_element_type=F32)
    lg_ref[...] = lg + br_ref[...]


def _out_router(mixpre, w_out, x2, mod6, norm2_g, w_router_p, b_router_p, seq):
    t, d = x2.shape
    tm = 256
    return pl.pallas_call(
        _out_router_kernel,
        grid=(t // tm,),
        in_specs=[pl.BlockSpec((tm, d), lambda i: (i, 0)),
                  pl.BlockSpec((d, d), lambda i: (0, 0)),
                  pl.BlockSpec((tm, d), lambda i: (i, 0)),
                  pl.BlockSpec((1, 6, d), lambda i: ((i * tm) // seq, 0, 0)),
                  pl.BlockSpec((1, d), lambda i: (0, 0)),
                  pl.BlockSpec((d, LANE), lambda i: (0, 0)),
                  pl.BlockSpec((1, LANE), lambda i: (0, 0))],
        out_specs=[pl.BlockSpec((tm, d), lambda i: (i, 0)),
                   pl.BlockSpec((tm, d), lambda i: (i, 0)),
                   pl.BlockSpec((tm, LANE), lambda i: (i, 0))],
        out_shape=[jax.ShapeDtypeStruct((t, d), F32),
                   jax.ShapeDtypeStruct((t, d), F32),
                   jax.ShapeDtypeStruct((t, LANE), F32)],
        compiler_params=_params("arbitrary"),
        name="out_proj_router",
    )(mixpre, w_out, x2, mod6, norm2_g.reshape(1, d), w_router_p, b_router_p)


DISPATCH_TOKENS = 512


def _pack_bf16_pairs(x):
    half = x.shape[1] // 2
    hi = lax.bitcast_convert_type(x[:, :half].astype(BF16).astype(F32), I32)
    lo = lax.bitcast_convert_type(x[:, half:].astype(BF16).astype(F32), I32)
    return hi | lax.shift_right_logical(lo, 16)


def _unpack_bf16_pairs(p):
    hi = lax.bitcast_convert_type(p & jnp.int32(-65536), F32).astype(BF16)
    lo = lax.bitcast_convert_type(lax.shift_left(p, 16), F32).astype(BF16)
    return hi, lo


def _dispatch_kernel(pos_ref, pad_ref, h_ref, xs_hbm, pk_scr, zero_scr, sem, *, n_alloc_blocks):
    i = pl.program_id(0)
    n_tiles = pl.num_programs(0)
    pk_scr[...] = _pack_bf16_pairs(h_ref[...])
    zero_scr[...] = jnp.zeros(zero_scr.shape, zero_scr.dtype)

    def row_copy(t, k):
        return pltpu.make_async_copy(
            pk_scr.at[pl.ds(t, 1), :],
            xs_hbm.at[pl.ds(pos_ref[0, 0, t * TOP_K + k], 1), :],
            sem)

    def pad_copy(r):
        return pltpu.make_async_copy(zero_scr.at[pl.ds(0, 1), :], xs_hbm.at[pl.ds(r, 1), :], sem)

    def tail_copy(b):
        off = pl.multiple_of(b * MOE_ROWS, MOE_ROWS)
        return pltpu.make_async_copy(zero_scr, xs_hbm.at[pl.ds(off, MOE_ROWS), :], sem)

    e_lo = (i * N_EXPERTS) // n_tiles
    e_hi = ((i + 1) * N_EXPERTS) // n_tiles
    tail_lo = jnp.where(i == 0, pad_ref[2, 0], n_alloc_blocks)

    def for_rows(fn):
        def per_token(t, carry):
            for k in range(TOP_K):
                fn(row_copy(t, k))
            return carry

        def per_expert(e, carry):
            def per_pad(r, c):
                fn(pad_copy(r))
                return c
            return lax.fori_loop(pad_ref[0, e], pad_ref[1, e], per_pad, carry)

        def per_tail(b, carry):
            fn(tail_copy(b))
            return carry

        lax.fori_loop(0, DISPATCH_TOKENS, per_token, 0)
        lax.fori_loop(e_lo, e_hi, per_expert, 0)
        lax.fori_loop(tail_lo, n_alloc_blocks, per_tail, 0)

    for_rows(lambda cp: cp.start())
    for_rows(lambda cp: cp.wait())


def _dispatch(pos, pad_rng, h2, n_alloc_rows):
    t, d = h2.shape
    n_tiles = t // DISPATCH_TOKENS
    pos3 = pos.reshape(n_tiles, 1, DISPATCH_TOKENS * TOP_K)
    return pl.pallas_call(
        functools.partial(_dispatch_kernel, n_alloc_blocks=n_alloc_rows // MOE_ROWS),
        grid=(n_tiles,),
        in_specs=[pl.BlockSpec((1, 1, DISPATCH_TOKENS * TOP_K), lambda i: (i, 0, 0),
                               memory_space=pltpu.SMEM),
                  pl.BlockSpec(memory_space=pltpu.SMEM),
                  pl.BlockSpec((DISPATCH_TOKENS, d), lambda i: (i, 0))],
        out_specs=pl.BlockSpec(memory_space=pl.ANY),
        out_shape=jax.ShapeDtypeStruct((n_alloc_rows, d // 2), I32),
        scratch_shapes=[pltpu.VMEM((DISPATCH_TOKENS, d // 2), I32),
                        pltpu.VMEM((MOE_ROWS, d // 2), I32),
                        pltpu.SemaphoreType.DMA(())],
        compiler_params=_params("arbitrary"),
        name="moe_dispatch",
    )(pos3, pad_rng, h2)


def _ffn_kernel(ie_ref, in_ref, ix_ref, io_ref, no_ref, x_ref, wg_ref, wl_ref, bg_ref, bl_ref,
                wd_ref, bd_ref, o_hbm, x_scr, acc_scr, sem):
    del ie_ref, ix_ref
    i = pl.program_id(0)
    f = pl.program_id(1)
    n_items = pl.num_programs(0)
    n_f = pl.num_programs(1)
    n_sub = in_ref[i]
    half = x_scr.shape[1] // 2
    r = MOE_ROWS
    o_ref = acc_scr.at[i % 2]

    def out_copy(item, s):
        off = pl.multiple_of(io_ref[item] * r, r)
        return pltpu.make_async_copy(acc_scr.at[item % 2, pl.ds(0, s * r), :],
                                     o_hbm.at[pl.ds(off, s * r), :], sem.at[item % 2])

    def for_out(item, fn):
        for s in range(1, ITEM_SUB + 1):
            @pl.when(no_ref[item] == s)
            def _():
                fn(out_copy(item, s))

    @pl.when(f == 0)
    def _():
        @pl.when(i >= 2)
        def _():
            for_out(i - 2, lambda cp: cp.wait())
        o_ref[...] = jnp.broadcast_to(bd_ref[0], o_ref.shape)

    def unpack(rows):
        hi, lo = _unpack_bf16_pairs(x_ref[:rows, :])
        x_scr[:rows, :half] = hi
        x_scr[:rows, half:] = lo

    def ffn(xb):
        g = jnp.dot(xb, wg_ref[0].astype(BF16), preferred_element_type=F32) + bg_ref[0]
        lin = jnp.dot(xb, wl_ref[0].astype(BF16), preferred_element_type=F32) + bl_ref[0]
        g = jnp.minimum(g, SWIGLU_LIMIT)
        lin = jnp.clip(lin, -SWIGLU_LIMIT, SWIGLU_LIMIT)
        act = g * jax.nn.sigmoid(SWIGLU_ALPHA * g) * (lin + 1.0)
        return jnp.dot(act.astype(BF16), wd_ref[0].astype(BF16), preferred_element_type=F32)

    for s in range(1, ITEM_SUB + 1):
        rows = s * r

        @pl.when(jnp.logical_and(n_sub == s, f == 0))
        def _():
            unpack(rows)

        @pl.when(n_sub == s)
        def _():
            o_ref[:rows, :] += ffn(x_scr[:rows, :])

    @pl.when(f == n_f - 1)
    def _():
        for_out(i, lambda cp: cp.start())

        @pl.when(i == n_items - 1)
        def _():
            @pl.when(i >= 1)
            def _():
                for_out(i - 1, lambda cp: cp.wait())
            for_out(i, lambda cp: cp.wait())


def _expert_ffn(items, xs, w_gate_up, b_gate_up3, w_down, b_down3):
    p, dh = xs.shape
    d = 2 * dh
    n_e, _, f2 = w_gate_up.shape
    d_ff = f2 // 2
    tf = MOE_FF_TILE
    nf = d_ff // tf
    n_items = items[0].shape[0]
    win = ITEM_SUB * MOE_ROWS
    grid_spec = pltpu.PrefetchScalarGridSpec(
        num_scalar_prefetch=5,
        grid=(n_items, nf),
        in_specs=[pl.BlockSpec((pl.Element(win), pl.Element(dh)),
                               lambda i, f, ie, ni, ix, io, no: (ix[i] * MOE_ROWS, 0)),
                  pl.BlockSpec((1, d, tf), lambda i, f, ie, ni, ix, io, no: (ie[i], 0, f)),
                  pl.BlockSpec((1, d, tf), lambda i, f, ie, ni, ix, io, no: (ie[i], 0, nf + f)),
                  pl.BlockSpec((1, 1, tf), lambda i, f, ie, ni, ix, io, no: (ie[i], 0, f)),
                  pl.BlockSpec((1, 1, tf), lambda i, f, ie, ni, ix, io, no: (ie[i], 0, nf + f)),
                  pl.BlockSpec((1, tf, d), lambda i, f, ie, ni, ix, io, no: (ie[i], f, 0)),
                  pl.BlockSpec((1, 1, d), lambda i, f, ie, ni, ix, io, no: (ie[i], 0, 0))],
        out_specs=pl.BlockSpec(memory_space=pl.ANY),
        scratch_shapes=[pltpu.VMEM((win, d), BF16),
                        pltpu.VMEM((2, win, d), F32),
                        pltpu.SemaphoreType.DMA((2,))],
    )
    return pl.pallas_call(
        _ffn_kernel,
        grid_spec=grid_spec,
        out_shape=jax.ShapeDtypeStruct((p, d), F32),
        compiler_params=_params("arbitrary", "arbitrary"),
        name="expert_ffn",
    )(*items, xs, w_gate_up, w_gate_up, b_gate_up3, b_gate_up3, w_down, b_down3)


COMBINE_TOKENS = 128


def _combine_kernel(pos_ref, ys_hbm, gate_ref, x1_ref, mod_ref, g_ref, o_ref, buf, sem):
    def row_copy(t, k):
        return pltpu.make_async_copy(
            ys_hbm.at[pl.ds(pos_ref[0, 0, t * TOP_K + k], 1), :],
            buf.at[k, pl.ds(t, 1), :],
            sem)

    def start(t, carry):
        for k in range(TOP_K):
            row_copy(t, k).start()
        return carry

    def wait(t, carry):
        for k in range(TOP_K):
            row_copy(t, k).wait()
        return carry

    lax.fori_loop(0, COMBINE_TOKENS, start, 0)
    lax.fori_loop(0, COMBINE_TOKENS, wait, 0)

    gate = gate_ref[...]
    moe = gate[:, 0:1] * buf[0]
    for k in range(1, TOP_K):
        moe = moe + gate[:, k:k + 1] * buf[k]
    x = x1_ref[...] + mod_ref[0, 5:6, :] * moe
    o_ref[...] = x * lax.rsqrt(jnp.mean(x * x, axis=-1, keepdims=True) + EPS) * g_ref[...]


def _combine(pos, ys, gates_p, x1, mod6, final_g, seq):
    t, d = x1.shape
    tt = COMBINE_TOKENS
    n_tiles = t // tt
    pos3 = pos.reshape(n_tiles, 1, tt * TOP_K)
    return pl.pallas_call(
        _combine_kernel,
        grid=(n_tiles,),
        in_specs=[pl.BlockSpec((1, 1, tt * TOP_K), lambda i: (i, 0, 0), memory_space=pltpu.SMEM),
                  pl.BlockSpec(memory_space=pl.ANY),
                  pl.BlockSpec((tt, LANE), lambda i: (i, 0)),
                  pl.BlockSpec((tt, d), lambda i: (i, 0)),
                  pl.BlockSpec((1, 6, d), lambda i: ((i * tt) // seq, 0, 0)),
                  pl.BlockSpec((1, d), lambda i: (0, 0))],
        out_specs=pl.BlockSpec((tt, d), lambda i: (i, 0)),
        out_shape=jax.ShapeDtypeStruct((t, d), F32),
        scratch_shapes=[pltpu.VMEM((TOP_K, tt, d), F32),
                        pltpu.SemaphoreType.DMA(())],
        compiler_params=_params("arbitrary"),
        name="moe_combine",
    )(pos3, ys, gates_p, x1, mod6, final_g.reshape(1, d))


def _routing(logits_p):
    t = logits_p.shape[0]
    logits = logits_p[:, :N_EXPERTS]
    top_vals, top_idx = lax.top_k(logits, TOP_K)
    gates = jax.nn.softmax(top_vals, axis=-1)
    onehot = jnp.sum((top_idx[:, :, None] == jnp.arange(N_EXPERTS, dtype=I32)).astype(I32), axis=1)
    rank = jnp.cumsum(onehot, axis=0) - onehot
    counts = jnp.sum(onehot, axis=0)
    padded = ((counts + MOE_ROWS - 1) // MOE_ROWS) * MOE_ROWS
    pad_ends = jnp.cumsum(padded)
    pstarts = pad_ends - padded
    pos = (pstarts[top_idx] + jnp.take_along_axis(rank, top_idx, axis=1)).astype(I32)
    gates_p = jnp.pad(gates, ((0, 0), (0, LANE - TOP_K)))

    win = ITEM_SUB * MOE_ROWS
    n_rows = t * TOP_K + N_EXPERTS * MOE_ROWS
    n_alloc = n_rows + win
    n_items = (n_rows // MOE_ROWS + N_EXPERTS * (ITEM_SUB - 1)) // ITEM_SUB + 1

    tail0 = pad_ends[-1]
    pad_rng = jnp.stack([pstarts + counts, pad_ends,
                         jnp.full((N_EXPERTS,), tail0 // MOE_ROWS)]).astype(I32)

    nb_e = padded // MOE_ROWS
    items_e = (nb_e + ITEM_SUB - 1) // ITEM_SUB
    item_ends = jnp.cumsum(items_e)
    item_starts = item_ends - items_e
    n_used = item_ends[-1]
    q = jnp.arange(n_items, dtype=I32)
    e_q = jnp.clip(jnp.searchsorted(item_ends, q, side="right"), 0, N_EXPERTS - 1)
    local = q - item_starts[e_q]
    used = q < n_used
    item_n = jnp.where(used, jnp.minimum(ITEM_SUB, nb_e[e_q] - ITEM_SUB * local), 0)
    item_x = jnp.where(used, pstarts[e_q] // MOE_ROWS + local * ITEM_SUB, 0)
    n_alloc_blocks = n_alloc // MOE_ROWS
    tail_blk = tail0 // MOE_ROWS + (q - n_used) * ITEM_SUB
    item_o = jnp.where(used, item_x, jnp.minimum(tail_blk, n_alloc_blocks - 1))
    item_no = jnp.where(used, item_n, jnp.clip(n_alloc_blocks - tail_blk, 0, ITEM_SUB))
    items = tuple(a.astype(I32) for a in (e_q, item_n, item_x, item_o, item_no))
    return pos, gates_p, pad_rng, items, n_alloc


def _layer(x2, c_act_in, bsz, seq, w_mod, b_mod, norm1_g, w_in, kv_norm_g, w_uk, w_uv,
           w_proj_a, sg_norm_g, w_spatial, b_spatial, w_proj_b, w_out, norm2_g,
           w_router, b_router, w_gate_up, b_gate_up, w_down, b_down):
    t, d = x2.shape
    w_qa = N_HEADS * HEAD_DIM
    w_qi = N_HEADS * IDX_DIM
    width_b = sg_norm_g.shape[0]

    mod = _modulation(c_act_in, w_mod, b_mod)
    mod6 = mod.reshape(bsz, 6, d)
    h = _norm_mod(x2, norm1_g, mod6[:, :3], seq)

    o0 = w_qa
    o1 = o0 + KV_LATENT
    o2 = o1 + w_qi
    o3 = o2 + IDX_DIM
    o4 = o3 + N_HEADS
    o5 = o4 + 2 * width_b
    w_small = jnp.concatenate(
        [w_in[:, o0:o1], w_in[:, o2:o3], w_in[:, o3:o4],
         jnp.zeros((d, LANE - N_HEADS), w_in.dtype)], axis=1).astype(BF16)

    q_a = _matmul(h, w_in[:, :o0].astype(BF16), name="proj_q_a")
    q_idx = _matmul(h, w_in[:, o1:o2].astype(BF16), name="proj_q_idx")
    z_b = _matmul(h, w_in[:, o4:o5].astype(BF16), name="proj_z_b")
    gates = _matmul(h, w_in[:, o5:].astype(BF16), act="sigmoid", name="proj_gates")
    c_kv, k_idx, w_idx = _small_proj(h, w_small, kv_norm_g)

    bias = _indexer(q_idx, k_idx.reshape(bsz, seq, IDX_DIM), w_idx, bsz, seq)
    o_a = _attention(q_a, c_kv.reshape(bsz, seq, KV_LATENT), bias,
                     w_uk.astype(BF16), w_uv.astype(BF16), bsz, seq)

    b_sp_t = jnp.pad(jnp.transpose(b_spatial), ((0, 0), (0, LANE - N_GROUPS_B)))
    sg = _spatial_gating(z_b, sg_norm_g, w_spatial, b_sp_t)

    mixpre = _mix(o_a, sg, w_proj_a.astype(BF16), w_proj_b.astype(BF16), gates)

    w_router_p = jnp.pad(w_router, ((0, 0), (0, LANE - N_EXPERTS)))
    b_router_p = jnp.pad(b_router, (0, LANE - N_EXPERTS)).reshape(1, LANE)
    x1, h2, logits_p = _out_router(mixpre, w_out.astype(BF16), x2, mod6, norm2_g,
                                   w_router_p, b_router_p, seq)

    pos, gates_p, pad_rng, items, n_alloc = _routing(logits_p)
    xs = _dispatch(pos.reshape(-1), pad_rng, h2, n_alloc)
    n_e = w_gate_up.shape[0]
    ys = _expert_ffn(items, xs, w_gate_up, b_gate_up.reshape(n_e, 1, -1),
                     w_down, b_down.reshape(n_e, 1, -1))
    return pos, ys, gates_p, x1, mod6


def kernel(x, c, w_mod, b_mod, norm1_g, w_in, kv_norm_g, w_uk, w_uv, w_proj_a, sg_norm_g,
           w_spatial, b_spatial, w_proj_b, w_out, norm2_g, w_router, b_router, w_gate_up,
           b_gate_up, w_down, b_down, final_g):
    bsz, seq, d = x.shape
    depth = w_mod.shape[0]
    assert depth == 1, "single-layer block"
    x2 = x.reshape(bsz * seq, d)
    pos, ys, gates_p, x1, mod6 = _layer(
        x2, c, bsz, seq, w_mod[0], b_mod[0], norm1_g[0], w_in[0], kv_norm_g[0], w_uk[0],
        w_uv[0], w_proj_a[0], sg_norm_g[0], w_spatial[0], b_spatial[0], w_proj_b[0],
        w_out[0], norm2_g[0], w_router[0], b_router[0], w_gate_up[0], b_gate_up[0],
        w_down[0], b_down[0])
    out = _combine(pos.reshape(-1), ys, gates_p, x1, mod6, final_g, seq)
    return out.reshape(bsz, seq, d)
```

```python
import functools

import jax
import jax.numpy as jnp
from jax import lax
from jax.experimental import pallas as pl
from jax.experimental.pallas import tpu as pltpu

F32 = jnp.float32
BF16 = jnp.bfloat16
I32 = jnp.int32

EPS = 1e-6
CHUNK = 64
N_HEADS = 16
HEAD_DIM = 128
KV_LATENT = 256
IDX_DIM = 128
TOPK_KEYS_MAX = 256
Q_BLOCK = 128
KEY_TILE = 256
N_GROUPS_B = 8
SG_CHUNK = 128
N_EXPERTS = 32
TOP_K = 4
SWIGLU_ALPHA = 1.702
SWIGLU_LIMIT = 7.0
ATTN_SCALE = HEAD_DIM ** -0.5
LOG2_E = 1.4426950408889634
IDX_W_SCALE = (N_HEADS ** -0.5) * (IDX_DIM ** -0.5)

LANE = 128
MOE_ROWS = 512
ITEM_SUB = 5
FFN_TILE = 256
NEG_BIAS = -1e30
INT_MIN = -2147483648
VMEM_LIMIT = 56 * 1024 * 1024


def _params(*sem):
    return pltpu.CompilerParams(dimension_semantics=sem, vmem_limit_bytes=VMEM_LIMIT)


def _mod_kernel(c_ref, w_ref, b_ref, o_ref):
    c = c_ref[...]
    ca = (c * jax.nn.sigmoid(c)).astype(BF16)
    o_ref[...] = jnp.dot(ca, w_ref[...].astype(BF16), preferred_element_type=F32) + b_ref[...]


def _modulation(c, w_mod, b_mod):
    bsz, d = c.shape
    n = w_mod.shape[1]
    tn = 1024
    return pl.pallas_call(
        _mod_kernel,
        grid=(n // tn,),
        in_specs=[pl.BlockSpec((bsz, d), lambda j: (0, 0)),
                  pl.BlockSpec((d, tn), lambda j: (0, j)),
                  pl.BlockSpec((1, tn), lambda j: (0, j))],
        out_specs=pl.BlockSpec((bsz, tn), lambda j: (0, j)),
        out_shape=jax.ShapeDtypeStruct((bsz, n), F32),
        compiler_params=_params("arbitrary"),
        name="modulation",
    )(c, w_mod, b_mod.reshape(1, n))


def _norm_mod_kernel(x_ref, g_ref, mod_ref, o_ref):
    x = x_ref[...]
    y = x * lax.rsqrt(jnp.mean(x * x, axis=-1, keepdims=True) + EPS) * g_ref[...]
    o_ref[...] = (y * (1.0 + mod_ref[0, 1:2, :]) + mod_ref[0, 0:1, :]).astype(o_ref.dtype)


def _norm_mod(x2, g, mod3, seq):
    t, d = x2.shape
    tm = 512
    return pl.pallas_call(
        _norm_mod_kernel,
        grid=(t // tm,),
        in_specs=[pl.BlockSpec((tm, d), lambda i: (i, 0)),
                  pl.BlockSpec((1, d), lambda i: (0, 0)),
                  pl.BlockSpec((1, 3, d), lambda i: ((i * tm) // seq, 0, 0))],
        out_specs=pl.BlockSpec((tm, d), lambda i: (i, 0)),
        out_shape=jax.ShapeDtypeStruct((t, d), BF16),
        compiler_params=_params("arbitrary"),
        name="norm_mod",
    )(x2, g.reshape(1, d), mod3)


def _mm_kernel(a_ref, w_ref, o_ref, *, act):
    acc = jnp.dot(a_ref[...], w_ref[...], preferred_element_type=F32)
    if act == "sigmoid":
        acc = jax.nn.sigmoid(acc)
    o_ref[...] = acc.astype(o_ref.dtype)


def _matmul(a, w, *, act=None, out_dtype=BF16, tm=1024, tn=1024, name="matmul"):
    m, k = a.shape
    n = w.shape[1]
    return pl.pallas_call(
        functools.partial(_mm_kernel, act=act),
        grid=(n // tn, m // tm),
        in_specs=[pl.BlockSpec((tm, k), lambda j, i: (i, 0)),
                  pl.BlockSpec((k, tn), lambda j, i: (0, j))],
        out_specs=pl.BlockSpec((tm, tn), lambda j, i: (i, j)),
        out_shape=jax.ShapeDtypeStruct((m, n), out_dtype),
        compiler_params=_params("arbitrary", "arbitrary"),
        name=name,
    )(a, w)


def _small_proj_kernel(a_ref, w_ref, g_ref, ckv_ref, kidx_ref, widx_ref):
    acc = jnp.dot(a_ref[...], w_ref[...], preferred_element_type=F32)
    c = acc[:, :KV_LATENT]
    cn = c * lax.rsqrt(jnp.mean(c * c, axis=-1, keepdims=True) + EPS) * g_ref[...]
    ckv_ref[...] = cn.astype(ckv_ref.dtype)
    kidx_ref[...] = acc[:, KV_LATENT:KV_LATENT + IDX_DIM].astype(kidx_ref.dtype)
    widx_ref[...] = acc[:, KV_LATENT + IDX_DIM:] * IDX_W_SCALE


def _small_proj(h, w_small, kv_g):
    t, d = h.shape
    n = w_small.shape[1]
    tm = 1024
    return pl.pallas_call(
        _small_proj_kernel,
        grid=(t // tm,),
        in_specs=[pl.BlockSpec((tm, d), lambda i: (i, 0)),
                  pl.BlockSpec((d, n), lambda i: (0, 0)),
                  pl.BlockSpec((1, KV_LATENT), lambda i: (0, 0))],
        out_specs=[pl.BlockSpec((tm, KV_LATENT), lambda i: (i, 0)),
                   pl.BlockSpec((tm, IDX_DIM), lambda i: (i, 0)),
                   pl.BlockSpec((tm, LANE), lambda i: (i, 0))],
        out_shape=[jax.ShapeDtypeStruct((t, KV_LATENT), BF16),
                   jax.ShapeDtypeStruct((t, IDX_DIM), BF16),
                   jax.ShapeDtypeStruct((t, LANE), F32)],
        compiler_params=_params("arbitrary"),
        name="small_proj",
    )(h, w_small, kv_g.reshape(1, KV_LATENT))


def _indexer_kernel(q_ref, k_ref, w_ref, bias_ref, key_scr, keyt_scr, wb_scr, *, n_kt, k_sel):
    i = pl.program_id(1)
    q = q_ref[...]
    qs = jnp.concatenate([q[:, h * IDX_DIM:(h + 1) * IDX_DIM] for h in range(N_HEADS)], axis=0)
    w = w_ref[...]
    for h in range(N_HEADS):
        wb_scr[h] = jnp.broadcast_to(w[:, h:h + 1], (Q_BLOCK, LANE))
    row = lax.broadcasted_iota(I32, (Q_BLOCK, KEY_TILE), 0)
    col = lax.broadcasted_iota(I32, (Q_BLOCK, KEY_TILE), 1)
    q_chunk = (i * Q_BLOCK + row) // CHUNK
    n_used = (i * Q_BLOCK) // KEY_TILE + 1

    for j in range(n_kt):
        @pl.when(j < n_used)
        def _():
            k = k_ref[0, j * KEY_TILE:(j + 1) * KEY_TILE, :]
            logits = lax.dot_general(qs, k, (((1,), (1,)), ((), ())),
                                     preferred_element_type=F32)
            def weighted(h):
                wb = wb_scr[h]
                wt = jnp.concatenate([wb] * (KEY_TILE // LANE), axis=1)
                return wt * jnp.maximum(logits[h * Q_BLOCK:(h + 1) * Q_BLOCK], 0.0)

            sc = weighted(0)
            for h in range(1, N_HEADS):
                sc = sc + weighted(h)
            bits = lax.bitcast_convert_type(sc, I32)
            key = bits ^ ((bits >> 31) & 0x7FFFFFFF)
            allowed = (j * KEY_TILE + col) // CHUNK <= q_chunk
            key = jnp.where(allowed, key, INT_MIN)
            key_scr[j] = key
            keyt_scr[j] = key.T

    def bisect(it, t_u):
        cand_u = t_u | jnp.left_shift(jnp.int32(1), 31 - it)
        cand_s = cand_u ^ INT_MIN

        def count(j, cnt):
            return cnt + jnp.where(keyt_scr[j] >= cand_s, 1.0, 0.0)

        cnt = lax.fori_loop(0, n_used, count, jnp.zeros((KEY_TILE, Q_BLOCK), F32))
        tot = jnp.sum(cnt, axis=0, keepdims=True)
        return jnp.where(tot >= float(k_sel), cand_u, t_u)

    t_u = lax.fori_loop(0, 32, bisect, jnp.zeros((1, Q_BLOCK), I32))
    t_s = jnp.maximum(t_u ^ INT_MIN, INT_MIN + 1)
    t_col = jnp.broadcast_to(t_s, (Q_BLOCK, Q_BLOCK)).T
    t_tile = jnp.concatenate([t_col] * (KEY_TILE // Q_BLOCK), axis=1)
    for j in range(n_kt):
        @pl.when(j < n_used)
        def _():
            bias_ref[0, 0, j] = jnp.where(key_scr[j] >= t_tile, 0.0, NEG_BIAS).astype(bias_ref.dtype)

        @pl.when(j >= n_used)
        def _():
            bias_ref[0, 0, j] = jnp.full((Q_BLOCK, KEY_TILE), NEG_BIAS, bias_ref.dtype)


def _indexer(q_idx, k_idx3, w_idx, bsz, seq):
    n_q = seq // Q_BLOCK
    n_kt = seq // KEY_TILE
    k_sel = min(TOPK_KEYS_MAX, seq // 4)
    return pl.pallas_call(
        functools.partial(_indexer_kernel, n_kt=n_kt, k_sel=k_sel),
        grid=(bsz, n_q),
        in_specs=[pl.BlockSpec((Q_BLOCK, N_HEADS * IDX_DIM), lambda b, i: (b * n_q + i, 0)),
                  pl.BlockSpec((1, seq, IDX_DIM), lambda b, i: (b, 0, 0)),
                  pl.BlockSpec((Q_BLOCK, LANE), lambda b, i: (b * n_q + i, 0))],
        out_specs=pl.BlockSpec((1, 1, n_kt, Q_BLOCK, KEY_TILE), lambda b, i: (b, i, 0, 0, 0)),
        out_shape=jax.ShapeDtypeStruct((bsz, n_q, n_kt, Q_BLOCK, KEY_TILE), BF16),
        scratch_shapes=[pltpu.VMEM((n_kt, Q_BLOCK, KEY_TILE), I32),
                        pltpu.VMEM((n_kt, KEY_TILE, Q_BLOCK), I32),
                        pltpu.VMEM((N_HEADS, Q_BLOCK, LANE), F32)],
        compiler_params=_params("arbitrary", "arbitrary"),
        name="indexer",
    )(q_idx, k_idx3, w_idx)


def _attn_kernel(qa_ref, c_ref, bias_ref, wuk_ref, wuv_ref, o_ref,
                 s_scr, q_scr, m_scr, l_scr, acc_scr):
    i = pl.program_id(1)
    n_used = (i * Q_BLOCK) // KEY_TILE + 1
    rows = N_HEADS * Q_BLOCK

    qa = qa_ref[...]
    for h in range(N_HEADS):
        ql = jnp.dot(qa[:, h * HEAD_DIM:(h + 1) * HEAD_DIM], wuk_ref[h],
                     preferred_element_type=F32)
        q_scr[h * Q_BLOCK:(h + 1) * Q_BLOCK, :] = (ql * (ATTN_SCALE * LOG2_E)).astype(BF16)

    m_scr[...] = jnp.full((rows, LANE), -jnp.inf, F32)

    def scores(j, carry):
        off = pl.multiple_of(j * KEY_TILE, KEY_TILE)
        k = c_ref[0, pl.ds(off, KEY_TILE), :]
        s = lax.dot_general(q_scr[...], k, (((1,), (1,)), ((), ())),
                            preferred_element_type=F32)
        b = bias_ref[0, 0, j].astype(F32)
        for h in range(N_HEADS):
            sl = slice(h * Q_BLOCK, (h + 1) * Q_BLOCK)
            sb = s[sl] + b
            s_scr[j, sl, :] = sb
            m_scr[sl, :] = jnp.maximum(m_scr[sl, :], jnp.maximum(sb[:, :LANE], sb[:, LANE:]))
        return carry

    lax.fori_loop(0, n_used, scores, 0)

    m = jnp.max(m_scr[...], axis=-1, keepdims=True)
    l_scr[...] = jnp.zeros((rows, LANE), F32)
    acc_scr[...] = jnp.zeros((rows, KV_LATENT), F32)

    def values(j, carry):
        off = pl.multiple_of(j * KEY_TILE, KEY_TILE)
        p = jnp.exp2(s_scr[j] - m)
        l_scr[...] += p[:, :LANE] + p[:, LANE:]
        acc_scr[...] += jnp.dot(p.astype(BF16), c_ref[0, pl.ds(off, KEY_TILE), :],
                                preferred_element_type=F32)
        return carry

    lax.fori_loop(0, n_used, values, 0)

    inv_l = 1.0 / jnp.sum(l_scr[...], axis=-1, keepdims=True)
    o = (acc_scr[...] * inv_l).astype(BF16)
    for h in range(N_HEADS):
        o_ref[:, h * HEAD_DIM:(h + 1) * HEAD_DIM] = jnp.dot(
            o[h * Q_BLOCK:(h + 1) * Q_BLOCK], wuv_ref[h],
            preferred_element_type=F32).astype(o_ref.dtype)


def _attention(q_a, c_kv3, bias, w_uk, w_uv, bsz, seq):
    n_q = seq // Q_BLOCK
    n_kt = seq // KEY_TILE
    rows = N_HEADS * Q_BLOCK
    t = bsz * seq
    return pl.pallas_call(
        _attn_kernel,
        grid=(bsz, n_q),
        in_specs=[pl.BlockSpec((Q_BLOCK, N_HEADS * HEAD_DIM), lambda b, i: (b * n_q + i, 0)),
                  pl.BlockSpec((1, seq, KV_LATENT), lambda b, i: (b, 0, 0)),
                  pl.BlockSpec((1, 1, n_kt, Q_BLOCK, KEY_TILE), lambda b, i: (b, i, 0, 0, 0)),
                  pl.BlockSpec((N_HEADS, HEAD_DIM, KV_LATENT), lambda b, i: (0, 0, 0)),
                  pl.BlockSpec((N_HEADS, KV_LATENT, HEAD_DIM), lambda b, i: (0, 0, 0))],
        out_specs=pl.BlockSpec((Q_BLOCK, N_HEADS * HEAD_DIM), lambda b, i: (b * n_q + i, 0)),
        out_shape=jax.ShapeDtypeStruct((t, N_HEADS * HEAD_DIM), BF16),
        scratch_shapes=[pltpu.VMEM((n_kt, rows, KEY_TILE), F32),
                        pltpu.VMEM((rows, KV_LATENT), BF16),
                        pltpu.VMEM((rows, LANE), F32),
                        pltpu.VMEM((rows, LANE), F32),
                        pltpu.VMEM((rows, KV_LATENT), F32)],
        compiler_params=_params("arbitrary", "arbitrary"),
        name="attention",
    )(q_a, c_kv3, bias, w_uk, w_uv)


def _spatial_kernel(z_ref, g_ref, ws_ref, bs_ref, o_ref, *, width):
    z = z_ref[...].astype(F32)
    z = 0.5 * z * (1.0 + lax.erf(z * (2.0 ** -0.5)))
    u = z[:, :width]
    v = z[:, width:]
    mu = jnp.mean(v, axis=-1, keepdims=True)
    vc = v - mu
    var = jnp.mean(vc * vc, axis=-1, keepdims=True)
    vn = (vc * lax.rsqrt(var + EPS) * g_ref[...]).astype(BF16)
    r = lax.broadcasted_iota(I32, (SG_CHUNK, SG_CHUNK), 0)
    c = lax.broadcasted_iota(I32, (SG_CHUNK, SG_CHUNK), 1)
    gd = width // N_GROUPS_B
    bs = bs_ref[...]
    for g in range(N_GROUPS_B):
        wg = jnp.where(r >= c, ws_ref[g], 0.0).astype(BF16)
        s = jnp.dot(wg, vn[:, g * gd:(g + 1) * gd], preferred_element_type=F32) + bs[:, g:g + 1]
        o_ref[:, g * gd:(g + 1) * gd] = (u[:, g * gd:(g + 1) * gd] * s).astype(o_ref.dtype)


def _spatial_gating(z, sg_g, w_spatial, b_spatial_t):
    t, w2 = z.shape
    width = w2 // 2
    return pl.pallas_call(
        functools.partial(_spatial_kernel, width=width),
        grid=(t // SG_CHUNK,),
        in_specs=[pl.BlockSpec((SG_CHUNK, w2), lambda i: (i, 0)),
                  pl.BlockSpec((1, width), lambda i: (0, 0)),
                  pl.BlockSpec((N_GROUPS_B, SG_CHUNK, SG_CHUNK), lambda i: (0, 0, 0)),
                  pl.BlockSpec((SG_CHUNK, LANE), lambda i: (0, 0))],
        out_specs=pl.BlockSpec((SG_CHUNK, width), lambda i: (i, 0)),
        out_shape=jax.ShapeDtypeStruct((t, width), BF16),
        compiler_params=_params("arbitrary"),
        name="spatial_gating",
    )(z, sg_g.reshape(1, width), w_spatial, b_spatial_t)


def _mix_kernel(a_ref, b_ref, wa_ref, wb_ref, ga_ref, gb_ref, o_ref):
    ya = jnp.dot(a_ref[...], wa_ref[...], preferred_element_type=F32)
    yb = jnp.dot(b_ref[...], wb_ref[...], preferred_element_type=F32)
    o_ref[...] = (ga_ref[...].astype(F32) * ya + gb_ref[...].astype(F32) * yb).astype(o_ref.dtype)


def _mix(o_a, sg, w_a, w_b, gates):
    t, k = o_a.shape
    d = w_a.shape[1]
    tm, tn = 1024, 1024
    nb = d // tn
    return pl.pallas_call(
        _mix_kernel,
        grid=(nb, t // tm),
        in_specs=[pl.BlockSpec((tm, k), lambda j, i: (i, 0)),
                  pl.BlockSpec((tm, k), lambda j, i: (i, 0)),
                  pl.BlockSpec((k, tn), lambda j, i: (0, j)),
                  pl.BlockSpec((k, tn), lambda j, i: (0, j)),
                  pl.BlockSpec((tm, tn), lambda j, i: (i, j)),
                  pl.BlockSpec((tm, tn), lambda j, i: (i, nb + j))],
        out_specs=pl.BlockSpec((tm, tn), lambda j, i: (i, j)),
        out_shape=jax.ShapeDtypeStruct((t, d), BF16),
        compiler_params=_params("arbitrary", "arbitrary"),
        name="branch_mix",
    )(o_a, sg, w_a, w_b, gates, gates)


def _out_router_kernel(a_ref, w_ref, x_ref, mod_ref, g_ref, wr_ref, br_ref,
                       x1_ref, h2_ref, lg_ref):
    mix = jnp.dot(a_ref[...], w_ref[...], preferred_element_type=F32)
    x1 = x_ref[...] + mod_ref[0, 2:3, :] * mix
    x1_ref[...] = x1
    y = x1 * lax.rsqrt(jnp.mean(x1 * x1, axis=-1, keepdims=True) + EPS) * g_ref[...]
    h2 = y * (1.0 + mod_ref[0, 4:5, :]) + mod_ref[0, 3:4, :]
    h2_ref[...] = h2
    h_hi = h2.astype(BF16)
    h_lo = (h2 - h_hi.astype(F32)).astype(BF16)
    wr = wr_ref[...]
    w_hi = wr.astype(BF16)
    w_lo = (wr - w_hi.astype(F32)).astype(BF16)
    lg = jnp.dot(h_hi, w_hi, preferred_element_type=F32)
    lg = lg + jnp.dot(h_lo, w_hi, preferred_element_type=F32)
    lg = lg + jnp.dot(h_hi, w_lo, preferred_element_type=F32)
    lg_ref[...] = lg + br_ref[...]


def _out_router(mixpre, w_out, x2, mod6, norm2_g, w_router_p, b_router_p, seq):
    t, d = x2.shape
    tm = 256
    return pl.pallas_call(
        _out_router_kernel,
        grid=(t // tm,),
        in_specs=[pl.BlockSpec((tm, d), lambda i: (i, 0)),
                  pl.BlockSpec((d, d), lambda i: (0, 0)),
                  pl.BlockSpec((tm, d), lambda i: (i, 0)),
                  pl.BlockSpec((1, 6, d), lambda i: ((i * tm) // seq, 0, 0)),
                  pl.BlockSpec((1, d), lambda i: (0, 0)),
                  pl.BlockSpec((d, LANE), lambda i: (0, 0)),
                  pl.BlockSpec((1, LANE), lambda i: (0, 0))],
        out_specs=[pl.BlockSpec((tm, d), lambda i: (i, 0)),
                   pl.BlockSpec((tm, d), lambda i: (i, 0)),
                   pl.BlockSpec((tm, LANE), lambda i: (i, 0))],
        out_shape=[jax.ShapeDtypeStruct((t, d), F32),
                   jax.ShapeDtypeStruct((t, d), F32),
                   jax.ShapeDtypeStruct((t, LANE), F32)],
        compiler_params=_params("arbitrary"),
        name="out_proj_router",
    )(mixpre, w_out, x2, mod6, norm2_g.reshape(1, d), w_router_p, b_router_p)


DISPATCH_TOKENS = 512


def _pack_bf16_pairs(x):
    half = x.shape[1] // 2
    hi = lax.bitcast_convert_type(x[:, :half].astype(BF16).astype(F32), I32)
    lo = lax.bitcast_convert_type(x[:, half:].astype(BF16).astype(F32), I32)
    return hi | lax.shift_right_logical(lo, 16)


def _unpack_bf16_pairs(p):
    hi = lax.bitcast_convert_type(p & jnp.int32(-65536), F32).astype(BF16)
    lo = lax.bitcast_convert_type(lax.shift_left(p, 16), F32).astype(BF16)
    return hi, lo


def _dispatch_kernel(pos_ref, pad_ref, h_ref, xs_hbm, pk_scr, zero_scr, sem, *, n_alloc_blocks):
    i = pl.program_id(0)
    n_tiles = pl.num_programs(0)
    pk_scr[...] = _pack_bf16_pairs(h_ref[...])
    zero_scr[...] = jnp.zeros(zero_scr.shape, zero_scr.dtype)

    def row_copy(t, k):
        return pltpu.make_async_copy(
            pk_scr.at[pl.ds(t, 1), :],
            xs_hbm.at[pl.ds(pos_ref[0, 0, t * TOP_K + k], 1), :],
            sem)

    def pad_copy(r):
        return pltpu.make_async_copy(zero_scr.at[pl.ds(0, 1), :], xs_hbm.at[pl.ds(r, 1), :], sem)

    def tail_copy(b):
        off = pl.multiple_of(b * MOE_ROWS, MOE_ROWS)
        return pltpu.make_async_copy(zero_scr, xs_hbm.at[pl.ds(off, MOE_ROWS), :], sem)

    e_lo = (i * N_EXPERTS) // n_tiles
    e_hi = ((i + 1) * N_EXPERTS) // n_tiles
    tail_lo = jnp.where(i == 0, pad_ref[2, 0], n_alloc_blocks)

    def for_rows(fn):
        def per_token(t, carry):
            for k in range(TOP_K):
                fn(row_copy(t, k))
            return carry

        def per_expert(e, carry):
            def per_pad(r, c):
                fn(pad_copy(r))
                return c
            return lax.fori_loop(pad_ref[0, e], pad_ref[1, e], per_pad, carry)

        def per_tail(b, carry):
            fn(tail_copy(b))
            return carry

        lax.fori_loop(0, DISPATCH_TOKENS, per_token, 0)
        lax.fori_loop(e_lo, e_hi, per_expert, 0)
        lax.fori_loop(tail_lo, n_alloc_blocks, per_tail, 0)

    for_rows(lambda cp: cp.start())
    for_rows(lambda cp: cp.wait())


def _dispatch(pos, pad_rng, h2, n_alloc_rows):
    t, d = h2.shape
    n_tiles = t // DISPATCH_TOKENS
    pos3 = pos.reshape(n_tiles, 1, DISPATCH_TOKENS * TOP_K)
    return pl.pallas_call(
        functools.partial(_dispatch_kernel, n_alloc_blocks=n_alloc_rows // MOE_ROWS),
        grid=(n_tiles,),
        in_specs=[pl.BlockSpec((1, 1, DISPATCH_TOKENS * TOP_K), lambda i: (i, 0, 0),
                               memory_space=pltpu.SMEM),
                  pl.BlockSpec(memory_space=pltpu.SMEM),
                  pl.BlockSpec((DISPATCH_TOKENS, d), lambda i: (i, 0))],
        out_specs=pl.BlockSpec(memory_space=pl.ANY),
        out_shape=jax.ShapeDtypeStruct((n_alloc_rows, d // 2), I32),
        scratch_shapes=[pltpu.VMEM((DISPATCH_TOKENS, d // 2), I32),
                        pltpu.VMEM((MOE_ROWS, d // 2), I32),
                        pltpu.SemaphoreType.DMA(())],
        compiler_params=_params("arbitrary"),
        name="moe_dispatch",
    )(pos3, pad_rng, h2)


def _ffn_kernel(ie_ref, in_ref, ix_ref, io_ref, no_ref, xs_hbm, wg_ref, wl_ref, bg_ref, bl_ref,
                wd_ref, bd_ref, o_hbm, x_scr, act_scr, stage_in, wg_b, wl_b, wd_b, stage_out,
                pend, sem_in, sem_out, *, n_fa):
    del ie_ref
    i = pl.program_id(0)
    s = pl.program_id(1)
    n_items = pl.num_programs(0)
    n_s = pl.num_programs(1)
    n_sub = in_ref[i]
    n_out = no_ref[i]
    half = x_scr.shape[1] // 2
    r = MOE_ROWS
    tile = FFN_TILE

    def rows_of(sb):
        return pl.ds(pl.multiple_of(sb * r, r), r)

    def in_copy(sb, slot):
        off = pl.multiple_of((ix_ref[i] + sb) * r, r)
        return pltpu.make_async_copy(xs_hbm.at[pl.ds(off, r), :], stage_in.at[slot],
                                     sem_in.at[slot])

    def out_copy(sb, slot, dt):
        off = pl.multiple_of((io_ref[i] + sb) * r, r)
        col = pl.multiple_of(dt * tile, tile)
        return pltpu.make_async_copy(stage_out.at[slot, rows_of(sb), :],
                                     o_hbm.at[pl.ds(off, r), pl.ds(col, tile)], sem_out.at[slot])

    def drain(slot):
        def body(k, c):
            out_copy(0, slot, 0).wait()
            return c
        lax.fori_loop(0, pend[slot], body, 0)
        pend[slot] = 0

    @pl.when(jnp.logical_and(i == 0, s == 0))
    def _():
        pend[0] = 0
        pend[1] = 0

    @pl.when(s == 0)
    def _():
        @pl.when(n_sub > 0)
        def _():
            in_copy(0, 0).start()

        def body(sb, c):
            slot = sb % 2
            in_copy(sb, slot).wait()

            @pl.when(sb + 1 < n_sub)
            def _():
                in_copy(sb + 1, 1 - slot).start()

            hi, lo = _unpack_bf16_pairs(stage_in[slot])
            x_scr[rows_of(sb), :half] = hi
            x_scr[rows_of(sb), half:] = lo
            return c

        lax.fori_loop(0, n_sub, body, 0)

    @pl.when(s < n_fa)
    def _():
        wg_b[...] = wg_ref[0].astype(BF16)
        wl_b[...] = wl_ref[0].astype(BF16)

        def body(sb, c):
            xb = x_scr[rows_of(sb), :]
            g = jnp.dot(xb, wg_b[...], preferred_element_type=F32) + bg_ref[0]
            lin = jnp.dot(xb, wl_b[...], preferred_element_type=F32) + bl_ref[0]
            g = jnp.minimum(g, SWIGLU_LIMIT)
            lin = jnp.clip(lin, -SWIGLU_LIMIT, SWIGLU_LIMIT)
            act = g * jax.nn.sigmoid(SWIGLU_ALPHA * g) * (lin + 1.0)
            act_scr[s, rows_of(sb), :] = act.astype(BF16)
            return c

        lax.fori_loop(0, n_sub, body, 0)

    @pl.when(s >= n_fa)
    def _():
        dt = s - n_fa
        slot = dt % 2
        drain(slot)
        wd_b[...] = wd_ref[0].astype(BF16)

        @pl.when(n_sub == 0)
        def _():
            stage_out[slot] = jnp.zeros(stage_out.shape[1:], stage_out.dtype)

        def body(sb, c):
            y = jnp.broadcast_to(bd_ref[0], (r, tile))
            for f in range(n_fa):
                y = y + jnp.dot(act_scr[f, rows_of(sb), :], wd_b[f * tile:(f + 1) * tile, :],
                                preferred_element_type=F32)
            stage_out[slot, rows_of(sb), :] = y
            return c

        lax.fori_loop(0, n_sub, body, 0)

        def send(sb, c):
            out_copy(sb, slot, dt).start()
            return c

        lax.fori_loop(0, n_out, send, 0)
        pend[slot] = n_out

        @pl.when(jnp.logical_and(i == n_items - 1, s == n_s - 1))
        def _():
            drain(0)
            drain(1)


def _expert_ffn(items, xs, w_gate_up, b_gate_up3, w_down, b_down3):
    p, dh = xs.shape
    d = 2 * dh
    n_e, _, f2 = w_gate_up.shape
    d_ff = f2 // 2
    tile = FFN_TILE
    n_fa = d_ff // tile
    n_db = d // tile
    n_items = items[0].shape[0]
    win = ITEM_SUB * MOE_ROWS

    def fa(s):
        return jnp.minimum(s, n_fa - 1)

    def db(s):
        return jnp.maximum(s - n_fa, 0)

    grid_spec = pltpu.PrefetchScalarGridSpec(
        num_scalar_prefetch=5,
        grid=(n_items, n_fa + n_db),
        in_specs=[pl.BlockSpec(memory_space=pl.ANY),
                  pl.BlockSpec((1, d, tile), lambda i, s, ie, ni, ix, io, no: (ie[i], 0, fa(s))),
                  pl.BlockSpec((1, d, tile), lambda i, s, ie, ni, ix, io, no: (ie[i], 0, n_fa + fa(s))),
                  pl.BlockSpec((1, 1, tile), lambda i, s, ie, ni, ix, io, no: (ie[i], 0, fa(s))),
                  pl.BlockSpec((1, 1, tile), lambda i, s, ie, ni, ix, io, no: (ie[i], 0, n_fa + fa(s))),
                  pl.BlockSpec((1, d_ff, tile), lambda i, s, ie, ni, ix, io, no: (ie[i], 0, db(s))),
                  pl.BlockSpec((1, 1, tile), lambda i, s, ie, ni, ix, io, no: (ie[i], 0, db(s)))],
        out_specs=pl.BlockSpec(memory_space=pl.ANY),
        scratch_shapes=[pltpu.VMEM((win, d), BF16),
                        pltpu.VMEM((n_fa, win, tile), BF16),
                        pltpu.VMEM((2, MOE_ROWS, dh), I32),
                        pltpu.VMEM((d, tile), BF16),
                        pltpu.VMEM((d, tile), BF16),
                        pltpu.VMEM((d_ff, tile), BF16),
                        pltpu.VMEM((2, win, tile), F32),
                        pltpu.SMEM((2,), I32),
                        pltpu.SemaphoreType.DMA((2,)),
                        pltpu.SemaphoreType.DMA((2,))],
    )
    return pl.pallas_call(
        functools.partial(_ffn_kernel, n_fa=n_fa),
        grid_spec=grid_spec,
        out_shape=jax.ShapeDtypeStruct((p, d), F32),
        compiler_params=_params("arbitrary", "arbitrary"),
        name="expert_ffn",
    )(*items, xs, w_gate_up, w_gate_up, b_gate_up3, b_gate_up3, w_down, b_down3)


COMBINE_TOKENS = 128


def _combine_kernel(pos_ref, ys_hbm, gate_ref, x1_ref, mod_ref, g_ref, o_ref, buf, sem):
    def row_copy(t, k):
        return pltpu.make_async_copy(
            ys_hbm.at[pl.ds(pos_ref[0, 0, t * TOP_K + k], 1), :],
            buf.at[k, pl.ds(t, 1), :],
            sem)

    def start(t, carry):
        for k in range(TOP_K):
            row_copy(t, k).start()
        return carry

    def wait(t, carry):
        for k in range(TOP_K):
            row_copy(t, k).wait()
        return carry

    lax.fori_loop(0, COMBINE_TOKENS, start, 0)
    lax.fori_loop(0, COMBINE_TOKENS, wait, 0)

    gate = gate_ref[...]
    moe = gate[:, 0:1] * buf[0]
    for k in range(1, TOP_K):
        moe = moe + gate[:, k:k + 1] * buf[k]
    x = x1_ref[...] + mod_ref[0, 5:6, :] * moe
    o_ref[...] = x * lax.rsqrt(jnp.mean(x * x, axis=-1, keepdims=True) + EPS) * g_ref[...]


def _combine(pos, ys, gates_p, x1, mod6, final_g, seq):
    t, d = x1.shape
    tt = COMBINE_TOKENS
    n_tiles = t // tt
    pos3 = pos.reshape(n_tiles, 1, tt * TOP_K)
    return pl.pallas_call(
        _combine_kernel,
        grid=(n_tiles,),
        in_specs=[pl.BlockSpec((1, 1, tt * TOP_K), lambda i: (i, 0, 0), memory_space=pltpu.SMEM),
                  pl.BlockSpec(memory_space=pl.ANY),
                  pl.BlockSpec((tt, LANE), lambda i: (i, 0)),
                  pl.BlockSpec((tt, d), lambda i: (i, 0)),
                  pl.BlockSpec((1, 6, d), lambda i: ((i * tt) // seq, 0, 0)),
                  pl.BlockSpec((1, d), lambda i: (0, 0))],
        out_specs=pl.BlockSpec((tt, d), lambda i: (i, 0)),
        out_shape=jax.ShapeDtypeStruct((t, d), F32),
        scratch_shapes=[pltpu.VMEM((TOP_K, tt, d), F32),
                        pltpu.SemaphoreType.DMA(())],
        compiler_params=_params("arbitrary"),
        name="moe_combine",
    )(pos3, ys, gates_p, x1, mod6, final_g.reshape(1, d))


def _routing(logits_p):
    t = logits_p.shape[0]
    logits = logits_p[:, :N_EXPERTS]
    top_vals, top_idx = lax.top_k(logits, TOP_K)
    gates = jax.nn.softmax(top_vals, axis=-1)
    onehot = jnp.sum((top_idx[:, :, None] == jnp.arange(N_EXPERTS, dtype=I32)).astype(I32), axis=1)
    rank = jnp.cumsum(onehot, axis=0) - onehot
    counts = jnp.sum(onehot, axis=0)
    padded = ((counts + MOE_ROWS - 1) // MOE_ROWS) * MOE_ROWS
    pad_ends = jnp.cumsum(padded)
    pstarts = pad_ends - padded
    pos = (pstarts[top_idx] + jnp.take_along_axis(rank, top_idx, axis=1)).astype(I32)
    gates_p = jnp.pad(gates, ((0, 0), (0, LANE - TOP_K)))

    n_alloc = t * TOP_K + N_EXPERTS * MOE_ROWS
    n_items = (n_alloc // MOE_ROWS + N_EXPERTS * (ITEM_SUB - 1)) // ITEM_SUB + 1

    tail0 = pad_ends[-1]
    pad_rng = jnp.stack([pstarts + counts, pad_ends,
                         jnp.full((N_EXPERTS,), tail0 // MOE_ROWS)]).astype(I32)

    nb_e = padded // MOE_ROWS
    items_e = (nb_e + ITEM_SUB - 1) // ITEM_SUB
    item_ends = jnp.cumsum(items_e)
    item_starts = item_ends - items_e
    n_used = item_ends[-1]
    q = jnp.arange(n_items, dtype=I32)
    e_q = jnp.clip(jnp.searchsorted(item_ends, q, side="right"), 0, N_EXPERTS - 1)
    local = q - item_starts[e_q]
    used = q < n_used
    item_n = jnp.where(used, jnp.minimum(ITEM_SUB, nb_e[e_q] - ITEM_SUB * local), 0)
    item_x = jnp.where(used, pstarts[e_q] // MOE_ROWS + local * ITEM_SUB, 0)
    n_alloc_blocks = n_alloc // MOE_ROWS
    tail_blk = tail0 // MOE_ROWS + (q - n_used) * ITEM_SUB
    item_o = jnp.where(used, item_x, jnp.minimum(tail_blk, n_alloc_blocks - 1))
    item_no = jnp.where(used, item_n, jnp.clip(n_alloc_blocks - tail_blk, 0, ITEM_SUB))
    items = tuple(a.astype(I32) for a in (e_q, item_n, item_x, item_o, item_no))
    return pos, gates_p, pad_rng, items, n_alloc


def _layer(x2, c_act_in, bsz, seq, w_mod, b_mod, norm1_g, w_in, kv_norm_g, w_uk, w_uv,
           w_proj_a, sg_norm_g, w_spatial, b_spatial, w_proj_b, w_out, norm2_g,
           w_router, b_router, w_gate_up, b_gate_up, w_down, b_down):
    t, d = x2.shape
    w_qa = N_HEADS * HEAD_DIM
    w_qi = N_HEADS * IDX_DIM
    width_b = sg_norm_g.shape[0]

    mod = _modulation(c_act_in, w_mod, b_mod)
    mod6 = mod.reshape(bsz, 6, d)
    h = _norm_mod(x2, norm1_g, mod6[:, :3], seq)

    o0 = w_qa
    o1 = o0 + KV_LATENT
    o2 = o1 + w_qi
    o3 = o2 + IDX_DIM
    o4 = o3 + N_HEADS
    o5 = o4 + 2 * width_b
    w_small = jnp.concatenate(
        [w_in[:, o0:o1], w_in[:, o2:o3], w_in[:, o3:o4],
         jnp.zeros((d, LANE - N_HEADS), w_in.dtype)], axis=1).astype(BF16)

    q_a = _matmul(h, w_in[:, :o0].astype(BF16), name="proj_q_a")
    q_idx = _matmul(h, w_in[:, o1:o2].astype(BF16), name="proj_q_idx")
    z_b = _matmul(h, w_in[:, o4:o5].astype(BF16), name="proj_z_b")
    gates = _matmul(h, w_in[:, o5:].astype(BF16), act="sigmoid", name="proj_gates")
    c_kv, k_idx, w_idx = _small_proj(h, w_small, kv_norm_g)

    bias = _indexer(q_idx, k_idx.reshape(bsz, seq, IDX_DIM), w_idx, bsz, seq)
    o_a = _attention(q_a, c_kv.reshape(bsz, seq, KV_LATENT), bias,
                     w_uk.astype(BF16), w_uv.astype(BF16), bsz, seq)

    b_sp_t = jnp.pad(jnp.transpose(b_spatial), ((0, 0), (0, LANE - N_GROUPS_B)))
    sg = _spatial_gating(z_b, sg_norm_g, w_spatial, b_sp_t)

    mixpre = _mix(o_a, sg, w_proj_a.astype(BF16), w_proj_b.astype(BF16), gates)

    w_router_p = jnp.pad(w_router, ((0, 0), (0, LANE - N_EXPERTS)))
    b_router_p = jnp.pad(b_router, (0, LANE - N_EXPERTS)).reshape(1, LANE)
    x1, h2, logits_p = _out_router(mixpre, w_out.astype(BF16), x2, mod6, norm2_g,
                                   w_router_p, b_router_p, seq)

    pos, gates_p, pad_rng, items, n_alloc = _routing(logits_p)
    xs = _dispatch(pos.reshape(-1), pad_rng, h2, n_alloc)
    n_e = w_gate_up.shape[0]
    ys = _expert_ffn(items, xs, w_gate_up, b_gate_up.reshape(n_e, 1, -1),
                     w_down, b_down.reshape(n_e, 1, -1))
    return pos, ys, gates_p, x1, mod6


def kernel(x, c, w_mod, b_mod, norm1_g, w_in, kv_norm_g, w_uk, w_uv, w_proj_a, sg_norm_g,
           w_spatial, b_spatial, w_proj_b, w_out, norm2_g, w_router, b_router, w_gate_up,
           b_gate_up, w_down, b_down, final_g):
    bsz, seq, d = x.shape
    depth = w_mod.shape[0]
    assert depth == 1, "single-layer block"
    x2 = x.reshape(bsz * seq, d)
    pos, ys, gates_p, x1, mod6 = _layer(
        x2, c, bsz, seq, w_mod[0], b_mod[0], norm1_g[0], w_in[0], kv_norm_g[0], w_uk[0],
        w_uv[0], w_proj_a[0], sg_norm_g[0], w_spatial[0], b_spatial[0], w_proj_b[0],
        w_out[0], norm2_g[0], w_router[0], b_router[0], w_gate_up[0], b_gate_up[0],
        w_down[0], b_down[0])
    out = _combine(pos.reshape(-1), ys, gates_p, x1, mod6, final_g, seq)
    return out.reshape(bsz, seq, d)
```

```python
import functools

import jax
import jax.numpy as jnp
from jax import lax
from jax.experimental import pallas as pl
from jax.experimental.pallas import tpu as pltpu

F32 = jnp.float32
BF16 = jnp.bfloat16
I32 = jnp.int32

EPS = 1e-6
CHUNK = 64
N_HEADS = 16
HEAD_DIM = 128
KV_LATENT = 256
IDX_DIM = 128
TOPK_KEYS_MAX = 256
Q_BLOCK = 128
KEY_TILE = 256
N_GROUPS_B = 8
SG_CHUNK = 128
N_EXPERTS = 32
TOP_K = 4
SWIGLU_ALPHA = 1.702
SWIGLU_LIMIT = 7.0
ATTN_SCALE = HEAD_DIM ** -0.5
LOG2_E = 1.4426950408889634
IDX_W_SCALE = (N_HEADS ** -0.5) * (IDX_DIM ** -0.5)

LANE = 128
MOE_ROWS = 512
ITEM_SUB = 2
FFN_TILE = 256
NEG_BIAS = -1e30
INT_MIN = -2147483648
VMEM_LIMIT = 56 * 1024 * 1024


def _params(*sem):
    return pltpu.CompilerParams(dimension_semantics=sem, vmem_limit_bytes=VMEM_LIMIT)


def _mod_kernel(c_ref, w_ref, b_ref, o_ref):
    c = c_ref[...]
    ca = (c * jax.nn.sigmoid(c)).astype(BF16)
    o_ref[...] = jnp.dot(ca, w_ref[...].astype(BF16), preferred_element_type=F32) + b_ref[...]


def _modulation(c, w_mod, b_mod):
    bsz, d = c.shape
    n = w_mod.shape[1]
    tn = 1024
    return pl.pallas_call(
        _mod_kernel,
        grid=(n // tn,),
        in_specs=[pl.BlockSpec((bsz, d), lambda j: (0, 0)),
                  pl.BlockSpec((d, tn), lambda j: (0, j)),
                  pl.BlockSpec((1, tn), lambda j: (0, j))],
        out_specs=pl.BlockSpec((bsz, tn), lambda j: (0, j)),
        out_shape=jax.ShapeDtypeStruct((bsz, n), F32),
        compiler_params=_params("arbitrary"),
        name="modulation",
    )(c, w_mod, b_mod.reshape(1, n))


def _norm_mod_kernel(x_ref, g_ref, mod_ref, o_ref):
    x = x_ref[...]
    y = x * lax.rsqrt(jnp.mean(x * x, axis=-1, keepdims=True) + EPS) * g_ref[...]
    o_ref[...] = (y * (1.0 + mod_ref[0, 1:2, :]) + mod_ref[0, 0:1, :]).astype(o_ref.dtype)


def _norm_mod(x2, g, mod3, seq):
    t, d = x2.shape
    tm = 512
    return pl.pallas_call(
        _norm_mod_kernel,
        grid=(t // tm,),
        in_specs=[pl.BlockSpec((tm, d), lambda i: (i, 0)),
                  pl.BlockSpec((1, d), lambda i: (0, 0)),
                  pl.BlockSpec((1, 3, d), lambda i: ((i * tm) // seq, 0, 0))],
        out_specs=pl.BlockSpec((tm, d), lambda i: (i, 0)),
        out_shape=jax.ShapeDtypeStruct((t, d), BF16),
        compiler_params=_params("arbitrary"),
        name="norm_mod",
    )(x2, g.reshape(1, d), mod3)


def _mm_kernel(a_ref, w_ref, o_ref, *, act):
    acc = jnp.dot(a_ref[...], w_ref[...], preferred_element_type=F32)
    if act == "sigmoid":
        acc = jax.nn.sigmoid(acc)
    o_ref[...] = acc.astype(o_ref.dtype)


def _matmul(a, w, *, act=None, out_dtype=BF16, tm=1024, tn=1024, name="matmul"):
    m, k = a.shape
    n = w.shape[1]
    return pl.pallas_call(
        functools.partial(_mm_kernel, act=act),
        grid=(n // tn, m // tm),
        in_specs=[pl.BlockSpec((tm, k), lambda j, i: (i, 0)),
                  pl.BlockSpec((k, tn), lambda j, i: (0, j))],
        out_specs=pl.BlockSpec((tm, tn), lambda j, i: (i, j)),
        out_shape=jax.ShapeDtypeStruct((m, n), out_dtype),
        compiler_params=_params("arbitrary", "arbitrary"),
        name=name,
    )(a, w)


def _small_proj_kernel(a_ref, w_ref, g_ref, ckv_ref, kidx_ref, widx_ref):
    acc = jnp.dot(a_ref[...], w_ref[...], preferred_element_type=F32)
    c = acc[:, :KV_LATENT]
    cn = c * lax.rsqrt(jnp.mean(c * c, axis=-1, keepdims=True) + EPS) * g_ref[...]
    ckv_ref[...] = cn.astype(ckv_ref.dtype)
    kidx_ref[...] = acc[:, KV_LATENT:KV_LATENT + IDX_DIM].astype(kidx_ref.dtype)
    widx_ref[...] = acc[:, KV_LATENT + IDX_DIM:] * IDX_W_SCALE


def _small_proj(h, w_small, kv_g):
    t, d = h.shape
    n = w_small.shape[1]
    tm = 1024
    return pl.pallas_call(
        _small_proj_kernel,
        grid=(t // tm,),
        in_specs=[pl.BlockSpec((tm, d), lambda i: (i, 0)),
                  pl.BlockSpec((d, n), lambda i: (0, 0)),
                  pl.BlockSpec((1, KV_LATENT), lambda i: (0, 0))],
        out_specs=[pl.BlockSpec((tm, KV_LATENT), lambda i: (i, 0)),
                   pl.BlockSpec((tm, IDX_DIM), lambda i: (i, 0)),
                   pl.BlockSpec((tm, LANE), lambda i: (i, 0))],
        out_shape=[jax.ShapeDtypeStruct((t, KV_LATENT), BF16),
                   jax.ShapeDtypeStruct((t, IDX_DIM), BF16),
                   jax.ShapeDtypeStruct((t, LANE), F32)],
        compiler_params=_params("arbitrary"),
        name="small_proj",
    )(h, w_small, kv_g.reshape(1, KV_LATENT))


def _indexer_kernel(q_ref, k_ref, w_ref, bias_ref, key_scr, keyt_scr, ngt_scr, run_scr,
                    *, n_kt, k_sel):
    i = pl.program_id(1)
    q = q_ref[...]
    qs = jnp.concatenate([q[:, h * IDX_DIM:(h + 1) * IDX_DIM] for h in range(N_HEADS)], axis=0)
    w = w_ref[...]
    wcols = [w[:, h:h + 1] for h in range(N_HEADS)]
    row = lax.broadcasted_iota(I32, (Q_BLOCK, KEY_TILE), 0)
    col = lax.broadcasted_iota(I32, (Q_BLOCK, KEY_TILE), 1)
    q_chunk = (i * Q_BLOCK + row) // CHUNK
    n_used = (i * Q_BLOCK) // KEY_TILE + 1

    for j in range(n_kt):
        @pl.when(j < n_used)
        def _():
            k = k_ref[0, j * KEY_TILE:(j + 1) * KEY_TILE, :]
            logits = lax.dot_general(qs, k, (((1,), (1,)), ((), ())),
                                     preferred_element_type=F32)
            sc = wcols[0] * jnp.maximum(logits[0:Q_BLOCK], 0.0)
            for h in range(1, N_HEADS):
                sc = sc + wcols[h] * jnp.maximum(logits[h * Q_BLOCK:(h + 1) * Q_BLOCK], 0.0)
            bits = lax.bitcast_convert_type(sc, I32)
            key = bits ^ ((bits >> 31) & 0x7FFFFFFF)
            allowed = (j * KEY_TILE + col) // CHUNK <= q_chunk
            key = jnp.where(allowed, key, INT_MIN)
            key_scr[j] = key
            keyt_scr[j] = key.T

    def bisect(it, t_u):
        cand_u = t_u | jnp.left_shift(jnp.int32(1), 31 - it)
        cand_s = cand_u ^ INT_MIN

        def count(j, cnt):
            return cnt + jnp.where(keyt_scr[j] >= cand_s, 1.0, 0.0)

        cnt = lax.fori_loop(0, n_used, count, jnp.zeros((KEY_TILE, Q_BLOCK), F32))
        tot = jnp.sum(cnt, axis=0, keepdims=True)
        return jnp.where(tot >= float(k_sel), cand_u, t_u)

    t_u = lax.fori_loop(0, 32, bisect, jnp.zeros((1, Q_BLOCK), I32))
    t_s = jnp.maximum(t_u ^ INT_MIN, INT_MIN + 1)
    t_col = jnp.broadcast_to(t_s, (Q_BLOCK, Q_BLOCK)).T
    t_tile = jnp.concatenate([t_col] * (KEY_TILE // Q_BLOCK), axis=1)

    def count_ge(j, cnt):
        return cnt + jnp.where(keyt_scr[j] >= t_s, 1.0, 0.0)

    n_ge = jnp.sum(lax.fori_loop(0, n_used, count_ge, jnp.zeros((KEY_TILE, Q_BLOCK), F32)),
                   axis=0, keepdims=True)
    has_ties = jnp.max(n_ge) > float(k_sel)

    def widen(x):
        return jnp.concatenate([x] * (KEY_TILE // LANE), axis=1)

    @pl.when(jnp.logical_not(has_ties))
    def _():
        for j in range(n_kt):
            @pl.when(j < n_used)
            def _():
                bias_ref[0, 0, j] = jnp.where(key_scr[j] >= t_tile, 0.0,
                                              NEG_BIAS).astype(bias_ref.dtype)

    @pl.when(has_ties)
    def _():
        ngt_scr[...] = jnp.zeros(ngt_scr.shape, F32)
        run_scr[...] = jnp.zeros(run_scr.shape, F32)
        for j in range(n_kt):
            @pl.when(j < n_used)
            def _():
                ngt_scr[...] += jnp.sum(jnp.where(key_scr[j] > t_tile, 1.0, 0.0),
                                        axis=-1, keepdims=True)
        a = lax.broadcasted_iota(I32, (KEY_TILE, KEY_TILE), 0)
        b = lax.broadcasted_iota(I32, (KEY_TILE, KEY_TILE), 1)
        before = jnp.where(a < b, 1.0, 0.0).astype(BF16)
        room = widen(float(k_sel) - ngt_scr[...])
        for j in range(n_kt):
            @pl.when(j < n_used)
            def _():
                key = key_scr[j]
                tie = jnp.where(key == t_tile, 1.0, 0.0)
                ahead = jnp.dot(tie.astype(BF16), before, preferred_element_type=F32) \
                    + widen(run_scr[...])
                keep_tie = jnp.where(ahead < room, tie, 0.0)
                keep = jnp.where(key > t_tile, 1.0, keep_tie)
                bias_ref[0, 0, j] = jnp.where(keep > 0.5, 0.0, NEG_BIAS).astype(bias_ref.dtype)
                run_scr[...] += jnp.sum(tie, axis=-1, keepdims=True)

    for j in range(n_kt):
        @pl.when(j >= n_used)
        def _():
            bias_ref[0, 0, j] = jnp.full((Q_BLOCK, KEY_TILE), NEG_BIAS, bias_ref.dtype)


def _indexer(q_idx, k_idx3, w_idx, bsz, seq):
    n_q = seq // Q_BLOCK
    n_kt = seq // KEY_TILE
    k_sel = min(TOPK_KEYS_MAX, seq // 4)
    return pl.pallas_call(
        functools.partial(_indexer_kernel, n_kt=n_kt, k_sel=k_sel),
        grid=(bsz, n_q),
        in_specs=[pl.BlockSpec((Q_BLOCK, N_HEADS * IDX_DIM), lambda b, i: (b * n_q + i, 0)),
                  pl.BlockSpec((1, seq, IDX_DIM), lambda b, i: (b, 0, 0)),
                  pl.BlockSpec((Q_BLOCK, LANE), lambda b, i: (b * n_q + i, 0))],
        out_specs=pl.BlockSpec((1, 1, n_kt, Q_BLOCK, KEY_TILE), lambda b, i: (b, i, 0, 0, 0)),
        out_shape=jax.ShapeDtypeStruct((bsz, n_q, n_kt, Q_BLOCK, KEY_TILE), BF16),
        scratch_shapes=[pltpu.VMEM((n_kt, Q_BLOCK, KEY_TILE), I32),
                        pltpu.VMEM((n_kt, KEY_TILE, Q_BLOCK), I32),
                        pltpu.VMEM((Q_BLOCK, LANE), F32),
                        pltpu.VMEM((Q_BLOCK, LANE), F32)],
        compiler_params=_params("arbitrary", "arbitrary"),
        name="indexer",
    )(q_idx, k_idx3, w_idx)


def _attn_kernel(qa_ref, c_ref, bias_ref, wuk_ref, wuv_ref, o_ref,
                 s_scr, q_scr, m_scr, l_scr, acc_scr):
    i = pl.program_id(1)
    n_used = (i * Q_BLOCK) // KEY_TILE + 1
    rows = N_HEADS * Q_BLOCK

    qa = qa_ref[...]
    for h in range(N_HEADS):
        ql = jnp.dot(qa[:, h * HEAD_DIM:(h + 1) * HEAD_DIM], wuk_ref[h],
                     preferred_element_type=F32)
        q_scr[h * Q_BLOCK:(h + 1) * Q_BLOCK, :] = (ql * (ATTN_SCALE * LOG2_E)).astype(BF16)

    m_scr[...] = jnp.full((rows, LANE), -jnp.inf, F32)

    def scores(j, carry):
        off = pl.multiple_of(j * KEY_TILE, KEY_TILE)
        k = c_ref[0, pl.ds(off, KEY_TILE), :]
        s = lax.dot_general(q_scr[...], k, (((1,), (1,)), ((), ())),
                            preferred_element_type=F32)
        b = bias_ref[0, 0, j].astype(F32)
        for h in range(N_HEADS):
            sl = slice(h * Q_BLOCK, (h + 1) * Q_BLOCK)
            sb = s[sl] + b
            s_scr[j, sl, :] = sb
            m_scr[sl, :] = jnp.maximum(m_scr[sl, :], jnp.maximum(sb[:, :LANE], sb[:, LANE:]))
        return carry

    lax.fori_loop(0, n_used, scores, 0)

    m = jnp.max(m_scr[...], axis=-1, keepdims=True)
    l_scr[...] = jnp.zeros((rows, LANE), F32)
    acc_scr[...] = jnp.zeros((rows, KV_LATENT), F32)

    def values(j, carry):
        off = pl.multiple_of(j * KEY_TILE, KEY_TILE)
        p = jnp.exp2(s_scr[j] - m)
        l_scr[...] += p[:, :LANE] + p[:, LANE:]
        acc_scr[...] += jnp.dot(p.astype(BF16), c_ref[0, pl.ds(off, KEY_TILE), :],
                                preferred_element_type=F32)
        return carry

    lax.fori_loop(0, n_used, values, 0)

    inv_l = 1.0 / jnp.sum(l_scr[...], axis=-1, keepdims=True)
    o = (acc_scr[...] * inv_l).astype(BF16)
    for h in range(N_HEADS):
        o_ref[:, h * HEAD_DIM:(h + 1) * HEAD_DIM] = jnp.dot(
            o[h * Q_BLOCK:(h + 1) * Q_BLOCK], wuv_ref[h],
            preferred_element_type=F32).astype(o_ref.dtype)


def _attention(q_a, c_kv3, bias, w_uk, w_uv, bsz, seq):
    n_q = seq // Q_BLOCK
    n_kt = seq // KEY_TILE
    rows = N_HEADS * Q_BLOCK
    t = bsz * seq
    return pl.pallas_call(
        _attn_kernel,
        grid=(bsz, n_q),
        in_specs=[pl.BlockSpec((Q_BLOCK, N_HEADS * HEAD_DIM), lambda b, i: (b * n_q + i, 0)),
                  pl.BlockSpec((1, seq, KV_LATENT), lambda b, i: (b, 0, 0)),
                  pl.BlockSpec((1, 1, n_kt, Q_BLOCK, KEY_TILE), lambda b, i: (b, i, 0, 0, 0)),
                  pl.BlockSpec((N_HEADS, HEAD_DIM, KV_LATENT), lambda b, i: (0, 0, 0)),
                  pl.BlockSpec((N_HEADS, KV_LATENT, HEAD_DIM), lambda b, i: (0, 0, 0))],
        out_specs=pl.BlockSpec((Q_BLOCK, N_HEADS * HEAD_DIM), lambda b, i: (b * n_q + i, 0)),
        out_shape=jax.ShapeDtypeStruct((t, N_HEADS * HEAD_DIM), BF16),
        scratch_shapes=[pltpu.VMEM((n_kt, rows, KEY_TILE), F32),
                        pltpu.VMEM((rows, KV_LATENT), BF16),
                        pltpu.VMEM((rows, LANE), F32),
                        pltpu.VMEM((rows, LANE), F32),
                        pltpu.VMEM((rows, KV_LATENT), F32)],
        compiler_params=_params("arbitrary", "arbitrary"),
        name="attention",
    )(q_a, c_kv3, bias, w_uk, w_uv)


def _spatial_kernel(z_ref, g_ref, ws_ref, bs_ref, o_ref, *, width):
    z = z_ref[...].astype(F32)
    z = 0.5 * z * (1.0 + lax.erf(z * (2.0 ** -0.5)))
    u = z[:, :width]
    v = z[:, width:]
    mu = jnp.mean(v, axis=-1, keepdims=True)
    vc = v - mu
    var = jnp.mean(vc * vc, axis=-1, keepdims=True)
    vn = (vc * lax.rsqrt(var + EPS) * g_ref[...]).astype(BF16)
    r = lax.broadcasted_iota(I32, (SG_CHUNK, SG_CHUNK), 0)
    c = lax.broadcasted_iota(I32, (SG_CHUNK, SG_CHUNK), 1)
    gd = width // N_GROUPS_B
    bs = bs_ref[...]
    for g in range(N_GROUPS_B):
        wg = jnp.where(r >= c, ws_ref[g], 0.0).astype(BF16)
        s = jnp.dot(wg, vn[:, g * gd:(g + 1) * gd], preferred_element_type=F32) + bs[:, g:g + 1]
        o_ref[:, g * gd:(g + 1) * gd] = (u[:, g * gd:(g + 1) * gd] * s).astype(o_ref.dtype)


def _spatial_gating(z, sg_g, w_spatial, b_spatial_t):
    t, w2 = z.shape
    width = w2 // 2
    return pl.pallas_call(
        functools.partial(_spatial_kernel, width=width),
        grid=(t // SG_CHUNK,),
        in_specs=[pl.BlockSpec((SG_CHUNK, w2), lambda i: (i, 0)),
                  pl.BlockSpec((1, width), lambda i: (0, 0)),
                  pl.BlockSpec((N_GROUPS_B, SG_CHUNK, SG_CHUNK), lambda i: (0, 0, 0)),
                  pl.BlockSpec((SG_CHUNK, LANE), lambda i: (0, 0))],
        out_specs=pl.BlockSpec((SG_CHUNK, width), lambda i: (i, 0)),
        out_shape=jax.ShapeDtypeStruct((t, width), BF16),
        compiler_params=_params("arbitrary"),
        name="spatial_gating",
    )(z, sg_g.reshape(1, width), w_spatial, b_spatial_t)


def _mix_kernel(a_ref, b_ref, wa_ref, wb_ref, ga_ref, gb_ref, o_ref):
    ya = jnp.dot(a_ref[...], wa_ref[...], preferred_element_type=F32)
    yb = jnp.dot(b_ref[...], wb_ref[...], preferred_element_type=F32)
    o_ref[...] = (ga_ref[...].astype(F32) * ya + gb_ref[...].astype(F32) * yb).astype(o_ref.dtype)


def _mix(o_a, sg, w_a, w_b, gates):
    t, k = o_a.shape
    d = w_a.shape[1]
    tm, tn = 1024, 1024
    nb = d // tn
    return pl.pallas_call(
        _mix_kernel,
        grid=(nb, t // tm),
        in_specs=[pl.BlockSpec((tm, k), lambda j, i: (i, 0)),
                  pl.BlockSpec((tm, k), lambda j, i: (i, 0)),
                  pl.BlockSpec((k, tn), lambda j, i: (0, j)),
                  pl.BlockSpec((k, tn), lambda j, i: (0, j)),
                  pl.BlockSpec((tm, tn), lambda j, i: (i, j)),
                  pl.BlockSpec((tm, tn), lambda j, i: (i, nb + j))],
        out_specs=pl.BlockSpec((tm, tn), lambda j, i: (i, j)),
        out_shape=jax.ShapeDtypeStruct((t, d), BF16),
        compiler_params=_params("arbitrary", "arbitrary"),
        name="branch_mix",
    )(o_a, sg, w_a, w_b, gates, gates)


def _out_router_kernel(a_ref, w_ref, x_ref, mod_ref, g_ref, wr_ref, br_ref,
                       x1_ref, h2_ref, lg_ref):
    mix = jnp.dot(a_ref[...], w_ref[...], preferred_element_type=F32)
    x1 = x_ref[...] + mod_ref[0, 2:3, :] * mix
    x1_ref[...] = x1
    y = x1 * lax.rsqrt(jnp.mean(x1 * x1, axis=-1, keepdims=True) + EPS) * g_ref[...]
    h2 = y * (1.0 + mod_ref[0, 4:5, :]) + mod_ref[0, 3:4, :]
    h2_ref[...] = h2
    h_hi = h2.astype(BF16)
    h_lo = (h2 - h_hi.astype(F32)).astype(BF16)
    wr = wr_ref[...]
    w_hi = wr.astype(BF16)
    w_lo = (wr - w_hi.astype(F32)).astype(BF16)
    lg = jnp.dot(h_hi, w_hi, preferred_element_type=F32)
    lg = lg + jnp.dot(h_lo, w_hi, preferred_element_type=F32)
    lg = lg + jnp.dot(h_hi, w_lo, preferred_element_type=F32)
    lg_ref[...] = lg + br_ref[...]


def _out_router(mixpre, w_out, x2, mod6, norm2_g, w_router_p, b_router_p, seq):
    t, d = x2.shape
    tm = 256
    return pl.pallas_call(
        _out_router_kernel,
        grid=(t // tm,),
        in_specs=[pl.BlockSpec((tm, d), lambda i: (i, 0)),
                  pl.BlockSpec((d, d), lambda i: (0, 0)),
                  pl.BlockSpec((tm, d), lambda i: (i, 0)),
                  pl.BlockSpec((1, 6, d), lambda i: ((i * tm) // seq, 0, 0)),
                  pl.BlockSpec((1, d), lambda i: (0, 0)),
                  pl.BlockSpec((d, LANE), lambda i: (0, 0)),
                  pl.BlockSpec((1, LANE), lambda i: (0, 0))],
        out_specs=[pl.BlockSpec((tm, d), lambda i: (i, 0)),
                   pl.BlockSpec((tm, d), lambda i: (i, 0)),
                   pl.BlockSpec((tm, LANE), lambda i: (i, 0))],
        out_shape=[jax.ShapeDtypeStruct((t, d), F32),
                   jax.ShapeDtypeStruct((t, d), F32),
                   jax.ShapeDtypeStruct((t, LANE), F32)],
        compiler_params=_params("arbitrary"),
        name="out_proj_router",
    )(mixpre, w_out, x2, mod6, norm2_g.reshape(1, d), w_router_p, b_router_p)


DISPATCH_TOKENS = 512


def _pack_bf16_pairs(x):
    half = x.shape[1] // 2
    hi = lax.bitcast_convert_type(x[:, :half].astype(BF16).astype(F32), I32)
    lo = lax.bitcast_convert_type(x[:, half:].astype(BF16).astype(F32), I32)
    return hi | lax.shift_right_logical(lo, 16)


def _unpack_bf16_pairs(p):
    hi = lax.bitcast_convert_type(p & jnp.int32(-65536), F32).astype(BF16)
    lo = lax.bitcast_convert_type(lax.shift_left(p, 16), F32).astype(BF16)
    return hi, lo


def _dispatch_kernel(pos_ref, pad_ref, h_ref, xs_hbm, pk_scr, zero_scr, sem, *, n_alloc_blocks):
    i = pl.program_id(0)
    n_tiles = pl.num_programs(0)
    pk_scr[...] = _pack_bf16_pairs(h_ref[...])
    zero_scr[...] = jnp.zeros(zero_scr.shape, zero_scr.dtype)

    def row_copy(t, k):
        return pltpu.make_async_copy(
            pk_scr.at[pl.ds(t, 1), :],
            xs_hbm.at[pl.ds(pos_ref[0, 0, t * TOP_K + k], 1), :],
            sem)

    def pad_copy(r):
        return pltpu.make_async_copy(zero_scr.at[pl.ds(0, 1), :], xs_hbm.at[pl.ds(r, 1), :], sem)

    def tail_copy(b):
        off = pl.multiple_of(b * MOE_ROWS, MOE_ROWS)
        return pltpu.make_async_copy(zero_scr, xs_hbm.at[pl.ds(off, MOE_ROWS), :], sem)

    e_lo = (i * N_EXPERTS) // n_tiles
    e_hi = ((i + 1) * N_EXPERTS) // n_tiles
    tail_lo = jnp.where(i == 0, pad_ref[2, 0], n_alloc_blocks)

    def for_rows(fn):
        def per_token(t, carry):
            for k in range(TOP_K):
                fn(row_copy(t, k))
            return carry

        def per_expert(e, carry):
            def per_pad(r, c):
                fn(pad_copy(r))
                return c
            return lax.fori_loop(pad_ref[0, e], pad_ref[1, e], per_pad, carry)

        def per_tail(b, carry):
            fn(tail_copy(b))
            return carry

        lax.fori_loop(0, DISPATCH_TOKENS, per_token, 0)
        lax.fori_loop(e_lo, e_hi, per_expert, 0)
        lax.fori_loop(tail_lo, n_alloc_blocks, per_tail, 0)

    for_rows(lambda cp: cp.start())
    for_rows(lambda cp: cp.wait())


def _dispatch(pos, pad_rng, h2, n_alloc_rows):
    t, d = h2.shape
    n_tiles = t // DISPATCH_TOKENS
    pos3 = pos.reshape(n_tiles, 1, DISPATCH_TOKENS * TOP_K)
    return pl.pallas_call(
        functools.partial(_dispatch_kernel, n_alloc_blocks=n_alloc_rows // MOE_ROWS),
        grid=(n_tiles,),
        in_specs=[pl.BlockSpec((1, 1, DISPATCH_TOKENS * TOP_K), lambda i: (i, 0, 0),
                               memory_space=pltpu.SMEM),
                  pl.BlockSpec(memory_space=pltpu.SMEM),
                  pl.BlockSpec((DISPATCH_TOKENS, d), lambda i: (i, 0))],
        out_specs=pl.BlockSpec(memory_space=pl.ANY),
        out_shape=jax.ShapeDtypeStruct((n_alloc_rows, d // 2), I32),
        scratch_shapes=[pltpu.VMEM((DISPATCH_TOKENS, d // 2), I32),
                        pltpu.VMEM((MOE_ROWS, d // 2), I32),
                        pltpu.SemaphoreType.DMA(())],
        compiler_params=_params("arbitrary"),
        name="moe_dispatch",
    )(pos3, pad_rng, h2)


def _ffn_kernel(ie_ref, in_ref, ix_ref, io_ref, no_ref, x_ref, wg_ref, wl_ref, bg_ref, bl_ref,
                wd_ref, bd_ref, o_hbm, x_scr, acc_scr, sem):
    del ie_ref, ix_ref
    i = pl.program_id(0)
    f = pl.program_id(1)
    n_items = pl.num_programs(0)
    n_f = pl.num_programs(1)
    n_sub = in_ref[i]
    half = x_scr.shape[1] // 2
    r = MOE_ROWS
    o_ref = acc_scr.at[i % 2]

    def out_copy(item, s):
        off = pl.multiple_of(io_ref[item] * r, r)
        return pltpu.make_async_copy(acc_scr.at[item % 2, pl.ds(0, s * r), :],
                                     o_hbm.at[pl.ds(off, s * r), :], sem.at[item % 2])

    def for_out(item, fn):
        for s in range(1, ITEM_SUB + 1):
            @pl.when(no_ref[item] == s)
            def _():
                fn(out_copy(item, s))

    @pl.when(f == 0)
    def _():
        @pl.when(i >= 2)
        def _():
            for_out(i - 2, lambda cp: cp.wait())
        o_ref[...] = jnp.broadcast_to(bd_ref[0], o_ref.shape)

    def unpack(rows):
        hi, lo = _unpack_bf16_pairs(x_ref[:rows, :])
        x_scr[:rows, :half] = hi
        x_scr[:rows, half:] = lo

    def ffn(xb):
        g = jnp.dot(xb, wg_ref[0].astype(BF16), preferred_element_type=F32) + bg_ref[0]
        lin = jnp.dot(xb, wl_ref[0].astype(BF16), preferred_element_type=F32) + bl_ref[0]
        g = jnp.minimum(g, SWIGLU_LIMIT)
        lin = jnp.clip(lin, -SWIGLU_LIMIT, SWIGLU_LIMIT)
        act = g * jax.nn.sigmoid(SWIGLU_ALPHA * g) * (lin + 1.0)
        return jnp.dot(act.astype(BF16), wd_ref[0].astype(BF16), preferred_element_type=F32)

    for s in range(1, ITEM_SUB + 1):
        rows = s * r

        @pl.when(jnp.logical_and(n_sub == s, f == 0))
        def _():
            unpack(rows)

        @pl.when(n_sub == s)
        def _():
            o_ref[:rows, :] += ffn(x_scr[:rows, :])

    @pl.when(f == n_f - 1)
    def _():
        for_out(i, lambda cp: cp.start())

        @pl.when(i == n_items - 1)
        def _():
            @pl.when(i >= 1)
            def _():
                for_out(i - 1, lambda cp: cp.wait())
            for_out(i, lambda cp: cp.wait())


def _expert_ffn(items, xs, w_gate_up, b_gate_up3, w_down, b_down3):
    p, dh = xs.shape
    d = 2 * dh
    n_e, _, f2 = w_gate_up.shape
    d_ff = f2 // 2
    tf = FFN_TILE
    nf = d_ff // tf
    n_items = items[0].shape[0]
    win = ITEM_SUB * MOE_ROWS

    def ft(f, ni, i):
        return jnp.where(ni[i] > 0, f, nf - 1)

    grid_spec = pltpu.PrefetchScalarGridSpec(
        num_scalar_prefetch=5,
        grid=(n_items, nf),
        in_specs=[pl.BlockSpec((pl.Element(win), pl.Element(dh)),
                               lambda i, f, ie, ni, ix, io, no: (ix[i] * MOE_ROWS, 0)),
                  pl.BlockSpec((1, d, tf), lambda i, f, ie, ni, ix, io, no: (ie[i], 0, ft(f, ni, i))),
                  pl.BlockSpec((1, d, tf),
                               lambda i, f, ie, ni, ix, io, no: (ie[i], 0, nf + ft(f, ni, i))),
                  pl.BlockSpec((1, 1, tf), lambda i, f, ie, ni, ix, io, no: (ie[i], 0, ft(f, ni, i))),
                  pl.BlockSpec((1, 1, tf),
                               lambda i, f, ie, ni, ix, io, no: (ie[i], 0, nf + ft(f, ni, i))),
                  pl.BlockSpec((1, tf, d), lambda i, f, ie, ni, ix, io, no: (ie[i], ft(f, ni, i), 0)),
                  pl.BlockSpec((1, 1, d), lambda i, f, ie, ni, ix, io, no: (ie[i], 0, 0))],
        out_specs=pl.BlockSpec(memory_space=pl.ANY),
        scratch_shapes=[pltpu.VMEM((win, d), BF16),
                        pltpu.VMEM((2, win, d), F32),
                        pltpu.SemaphoreType.DMA((2,))],
    )
    return pl.pallas_call(
        _ffn_kernel,
        grid_spec=grid_spec,
        out_shape=jax.ShapeDtypeStruct((p, d), F32),
        compiler_params=_params("arbitrary", "arbitrary"),
        name="expert_ffn",
    )(*items, xs, w_gate_up, w_gate_up, b_gate_up3, b_gate_up3, w_down, b_down3)


COMBINE_TOKENS = 128


def _combine_kernel(pos_ref, pos_next_ref, ys_hbm, gate_ref, x1_ref, mod_ref, g_ref, o_ref,
                    buf, sem):
    i = pl.program_id(0)
    n = pl.num_programs(0)
    slot = i % 2

    def issue(p_ref, sl):
        def body(t, carry):
            for k in range(TOP_K):
                pltpu.make_async_copy(ys_hbm.at[pl.ds(p_ref[0, 0, t * TOP_K + k], 1), :],
                                      buf.at[sl, k, pl.ds(t, 1), :], sem.at[sl]).start()
            return carry
        lax.fori_loop(0, COMBINE_TOKENS, body, 0)

    def wait_all(sl):
        def body(t, carry):
            for k in range(TOP_K):
                pltpu.make_async_copy(ys_hbm.at[pl.ds(0, 1), :],
                                      buf.at[sl, k, pl.ds(t, 1), :], sem.at[sl]).wait()
            return carry
        lax.fori_loop(0, COMBINE_TOKENS, body, 0)

    @pl.when(i == 0)
    def _():
        issue(pos_ref, 0)

    @pl.when(i + 1 < n)
    def _():
        issue(pos_next_ref, 1 - slot)

    wait_all(slot)

    gate = gate_ref[...]
    moe = gate[:, 0:1] * buf[slot, 0]
    for k in range(1, TOP_K):
        moe = moe + gate[:, k:k + 1] * buf[slot, k]
    x = x1_ref[...] + mod_ref[0, 5:6, :] * moe
    o_ref[...] = x * lax.rsqrt(jnp.mean(x * x, axis=-1, keepdims=True) + EPS) * g_ref[...]


def _combine(pos, ys, gates_p, x1, mod6, final_g, seq):
    t, d = x1.shape
    tt = COMBINE_TOKENS
    n_tiles = t // tt
    pos3 = pos.reshape(n_tiles, 1, tt * TOP_K)
    return pl.pallas_call(
        _combine_kernel,
        grid=(n_tiles,),
        in_specs=[pl.BlockSpec((1, 1, tt * TOP_K), lambda i: (i, 0, 0), memory_space=pltpu.SMEM),
                  pl.BlockSpec((1, 1, tt * TOP_K), lambda i: (jnp.minimum(i + 1, n_tiles - 1), 0, 0),
                               memory_space=pltpu.SMEM),
                  pl.BlockSpec(memory_space=pl.ANY),
                  pl.BlockSpec((tt, LANE), lambda i: (i, 0)),
                  pl.BlockSpec((tt, d), lambda i: (i, 0)),
                  pl.BlockSpec((1, 6, d), lambda i: ((i * tt) // seq, 0, 0)),
                  pl.BlockSpec((1, d), lambda i: (0, 0))],
        out_specs=pl.BlockSpec((tt, d), lambda i: (i, 0)),
        out_shape=jax.ShapeDtypeStruct((t, d), F32),
        scratch_shapes=[pltpu.VMEM((2, TOP_K, tt, d), F32),
                        pltpu.SemaphoreType.DMA((2,))],
        compiler_params=_params("arbitrary"),
        name="moe_combine",
    )(pos3, pos3, ys, gates_p, x1, mod6, final_g.reshape(1, d))


ROUTE_TOKENS = 512


def _route_kernel(lg_ref, idx_ref, gate_ref, rank_ref, cnt_ref, carry_scr):
    i = pl.program_id(0)
    tt = lg_ref.shape[0]

    @pl.when(i == 0)
    def _():
        carry_scr[...] = jnp.zeros(carry_scr.shape, F32)

    lane = lax.broadcasted_iota(I32, (tt, LANE), 1)
    v = jnp.where(lane < N_EXPERTS, lg_ref[...], -jnp.inf)
    tops, ids, hits = [], [], []
    for _ in range(TOP_K):
        m = jnp.max(v, axis=-1, keepdims=True)
        idx = jnp.min(jnp.where(v == m, lane, LANE), axis=-1, keepdims=True)
        hit = lane == idx
        v = jnp.where(hit, -jnp.inf, v)
        tops.append(m)
        ids.append(idx)
        hits.append(hit)

    exps = [jnp.exp(m - tops[0]) for m in tops]
    total = exps[0]
    for e in exps[1:]:
        total = total + e

    sel = jnp.where(hits[0], 1.0, 0.0)
    for hit in hits[1:]:
        sel = sel + jnp.where(hit, 1.0, 0.0)
    r = lax.broadcasted_iota(I32, (tt, tt), 0)
    c = lax.broadcasted_iota(I32, (tt, tt), 1)
    earlier = jnp.where(c < r, 1.0, 0.0).astype(BF16)
    rank_all = jnp.dot(earlier, sel.astype(BF16), preferred_element_type=F32) + carry_scr[...]
    carry_scr[...] += jnp.sum(sel, axis=0, keepdims=True)
    cnt_ref[...] = carry_scr[...]

    idx_out = jnp.zeros((tt, LANE), I32)
    gate_out = jnp.zeros((tt, LANE), F32)
    rank_out = jnp.zeros((tt, LANE), F32)
    for k in range(TOP_K):
        rk = jnp.sum(jnp.where(hits[k], rank_all, 0.0), axis=-1, keepdims=True)
        idx_out = jnp.where(lane == k, ids[k], idx_out)
        gate_out = jnp.where(lane == k, exps[k] / total, gate_out)
        rank_out = jnp.where(lane == k, rk, rank_out)
    idx_ref[...] = idx_out
    gate_ref[...] = gate_out
    rank_ref[...] = rank_out.astype(I32)


def _route(logits_p):
    t = logits_p.shape[0]
    tt = ROUTE_TOKENS
    return pl.pallas_call(
        _route_kernel,
        grid=(t // tt,),
        in_specs=[pl.BlockSpec((tt, LANE), lambda i: (i, 0))],
        out_specs=[pl.BlockSpec((tt, LANE), lambda i: (i, 0)),
                   pl.BlockSpec((tt, LANE), lambda i: (i, 0)),
                   pl.BlockSpec((tt, LANE), lambda i: (i, 0)),
                   pl.BlockSpec((1, LANE), lambda i: (0, 0))],
        out_shape=[jax.ShapeDtypeStruct((t, LANE), I32),
                   jax.ShapeDtypeStruct((t, LANE), F32),
                   jax.ShapeDtypeStruct((t, LANE), I32),
                   jax.ShapeDtypeStruct((1, LANE), F32)],
        scratch_shapes=[pltpu.VMEM((1, LANE), F32)],
        compiler_params=_params("arbitrary"),
        name="moe_route",
    )(logits_p)


def _routing(logits_p):
    t = logits_p.shape[0]
    idx_p, gates_p, rank_p, cnt = _route(logits_p)
    top_idx = idx_p[:, :TOP_K]
    counts = cnt[0, :N_EXPERTS].astype(I32)
    padded = ((counts + MOE_ROWS - 1) // MOE_ROWS) * MOE_ROWS
    pad_ends = jnp.cumsum(padded)
    pstarts = pad_ends - padded
    pos = (pstarts[top_idx] + rank_p[:, :TOP_K]).astype(I32)

    win = ITEM_SUB * MOE_ROWS
    n_rows = t * TOP_K + N_EXPERTS * MOE_ROWS
    n_alloc = n_rows + win
    n_items = (n_rows // MOE_ROWS + N_EXPERTS * (ITEM_SUB - 1)) // ITEM_SUB + 1

    tail0 = pad_ends[-1]
    pad_rng = jnp.stack([pstarts + counts, pad_ends,
                         jnp.full((N_EXPERTS,), tail0 // MOE_ROWS)]).astype(I32)

    nb_e = padded // MOE_ROWS
    items_e = (nb_e + ITEM_SUB - 1) // ITEM_SUB
    item_ends = jnp.cumsum(items_e)
    item_starts = item_ends - items_e
    n_used = item_ends[-1]
    q = jnp.arange(n_items, dtype=I32)
    e_q = jnp.clip(jnp.searchsorted(item_ends, q, side="right"), 0, N_EXPERTS - 1)
    local = q - item_starts[e_q]
    used = q < n_used
    item_n = jnp.where(used, jnp.minimum(ITEM_SUB, nb_e[e_q] - ITEM_SUB * local), 0)
    item_x = jnp.where(used, pstarts[e_q] // MOE_ROWS + local * ITEM_SUB, 0)
    n_alloc_blocks = n_alloc // MOE_ROWS
    tail_blk = tail0 // MOE_ROWS + (q - n_used) * ITEM_SUB
    item_o = jnp.where(used, item_x, jnp.minimum(tail_blk, n_alloc_blocks - 1))
    item_no = jnp.where(used, item_n, jnp.clip(n_alloc_blocks - tail_blk, 0, ITEM_SUB))
    e_last = jnp.max(jnp.where(items_e > 0, jnp.arange(N_EXPERTS, dtype=I32), 0))
    item_e = jnp.where(used, e_q, e_last)
    items = tuple(a.astype(I32) for a in (item_e, item_n, item_x, item_o, item_no))
    return pos, gates_p, pad_rng, items, n_alloc


def _layer(x2, c_act_in, bsz, seq, w_mod, b_mod, norm1_g, w_in, kv_norm_g, w_uk, w_uv,
           w_proj_a, sg_norm_g, w_spatial, b_spatial, w_proj_b, w_out, norm2_g,
           w_router, b_router, w_gate_up, b_gate_up, w_down, b_down):
    t, d = x2.shape
    w_qa = N_HEADS * HEAD_DIM
    w_qi = N_HEADS * IDX_DIM
    width_b = sg_norm_g.shape[0]

    mod = _modulation(c_act_in, w_mod, b_mod)
    mod6 = mod.reshape(bsz, 6, d)
    h = _norm_mod(x2, norm1_g, mod6[:, :3], seq)

    o0 = w_qa
    o1 = o0 + KV_LATENT
    o2 = o1 + w_qi
    o3 = o2 + IDX_DIM
    o4 = o3 + N_HEADS
    o5 = o4 + 2 * width_b
    w_small = jnp.concatenate(
        [w_in[:, o0:o1], w_in[:, o2:o3], w_in[:, o3:o4],
         jnp.zeros((d, LANE - N_HEADS), w_in.dtype)], axis=1).astype(BF16)

    q_a = _matmul(h, w_in[:, :o0].astype(BF16), name="proj_q_a")
    q_idx = _matmul(h, w_in[:, o1:o2].astype(BF16), name="proj_q_idx")
    z_b = _matmul(h, w_in[:, o4:o5].astype(BF16), name="proj_z_b")
    gates = _matmul(h, w_in[:, o5:].astype(BF16), act="sigmoid", name="proj_gates")
    c_kv, k_idx, w_idx = _small_proj(h, w_small, kv_norm_g)

    bias = _indexer(q_idx, k_idx.reshape(bsz, seq, IDX_DIM), w_idx, bsz, seq)
    o_a = _attention(q_a, c_kv.reshape(bsz, seq, KV_LATENT), bias,
                     w_uk.astype(BF16), w_uv.astype(BF16), bsz, seq)

    b_sp_t = jnp.pad(jnp.transpose(b_spatial), ((0, 0), (0, LANE - N_GROUPS_B)))
    sg = _spatial_gating(z_b, sg_norm_g, w_spatial, b_sp_t)

    mixpre = _mix(o_a, sg, w_proj_a.astype(BF16), w_proj_b.astype(BF16), gates)

    w_router_p = jnp.pad(w_router, ((0, 0), (0, LANE - N_EXPERTS)))
    b_router_p = jnp.pad(b_router, (0, LANE - N_EXPERTS)).reshape(1, LANE)
    x1, h2, logits_p = _out_router(mixpre, w_out.astype(BF16), x2, mod6, norm2_g,
                                   w_router_p, b_router_p, seq)

    pos, gates_p, pad_rng, items, n_alloc = _routing(logits_p)
    xs = _dispatch(pos.reshape(-1), pad_rng, h2, n_alloc)
    n_e = w_gate_up.shape[0]
    ys = _expert_ffn(items, xs, w_gate_up, b_gate_up.reshape(n_e, 1, -1),
                     w_down, b_down.reshape(n_e, 1, -1))
    return pos, ys, gates_p, x1, mod6


def kernel(x, c, w_mod, b_mod, norm1_g, w_in, kv_norm_g, w_uk, w_uv, w_proj_a, sg_norm_g,
           w_spatial, b_spatial, w_proj_b, w_out, norm2_g, w_router, b_router, w_gate_up,
           b_gate_up, w_down, b_down, final_g):
    bsz, seq, d = x.shape
    depth = w_mod.shape[0]
    assert depth == 1, "single-layer block"
    x2 = x.reshape(bsz * seq, d)
    pos, ys, gates_p, x1, mod6 = _layer(
        x2, c, bsz, seq, w_mod[0], b_mod[0], norm1_g[0], w_in[0], kv_norm_g[0], w_uk[0],
        w_uv[0], w_proj_a[0], sg_norm_g[0], w_spatial[0], b_spatial[0], w_proj_b[0],
        w_out[0], norm2_g[0], w_router[0], b_router[0], w_gate_up[0], b_gate_up[0],
        w_down[0], b_down[0])
    out = _combine(pos.reshape(-1), ys, gates_p, x1, mod6, final_g, seq)
    return out.reshape(bsz, seq, d)
```

```python
import functools

import jax
import jax.numpy as jnp
from jax import lax
from jax.experimental import pallas as pl
from jax.experimental.pallas import tpu as pltpu

F32 = jnp.float32
BF16 = jnp.bfloat16
I32 = jnp.int32

EPS = 1e-6
CHUNK = 64
N_HEADS = 16
HEAD_DIM = 128
KV_LATENT = 256
IDX_DIM = 128
TOPK_KEYS_MAX = 256
Q_BLOCK = 128
KEY_TILE = 256
N_GROUPS_B = 8
SG_CHUNK = 128
N_EXPERTS = 32
TOP_K = 4
SWIGLU_ALPHA = 1.702
SWIGLU_LIMIT = 7.0
ATTN_SCALE = HEAD_DIM ** -0.5
LOG2_E = 1.4426950408889634
IDX_W_SCALE = (N_HEADS ** -0.5) * (IDX_DIM ** -0.5)

LANE = 128
MOE_ROWS = 512
ITEM_SUB = 2
FFN_TILE = 256
NEG_BIAS = -1e30
INT_MIN = -2147483648
VMEM_LIMIT = 56 * 1024 * 1024


def _params(*sem):
    return pltpu.CompilerParams(dimension_semantics=sem, vmem_limit_bytes=VMEM_LIMIT)


def _mod_kernel(c_ref, w_ref, b_ref, o_ref):
    c = c_ref[...]
    ca = (c * jax.nn.sigmoid(c)).astype(BF16)
    o_ref[...] = jnp.dot(ca, w_ref[...].astype(BF16), preferred_element_type=F32) + b_ref[...]


def _modulation(c, w_mod, b_mod):
    bsz, d = c.shape
    n = w_mod.shape[1]
    tn = 1024
    return pl.pallas_call(
        _mod_kernel,
        grid=(n // tn,),
        in_specs=[pl.BlockSpec((bsz, d), lambda j: (0, 0)),
                  pl.BlockSpec((d, tn), lambda j: (0, j)),
                  pl.BlockSpec((1, tn), lambda j: (0, j))],
        out_specs=pl.BlockSpec((bsz, tn), lambda j: (0, j)),
        out_shape=jax.ShapeDtypeStruct((bsz, n), F32),
        compiler_params=_params("arbitrary"),
        name="modulation",
    )(c, w_mod, b_mod.reshape(1, n))


def _norm_mod_kernel(x_ref, g_ref, mod_ref, o_ref):
    x = x_ref[...]
    y = x * lax.rsqrt(jnp.mean(x * x, axis=-1, keepdims=True) + EPS) * g_ref[...]
    o_ref[...] = (y * (1.0 + mod_ref[0, 1:2, :]) + mod_ref[0, 0:1, :]).astype(o_ref.dtype)


def _norm_mod(x2, g, mod3, seq):
    t, d = x2.shape
    tm = 512
    return pl.pallas_call(
        _norm_mod_kernel,
        grid=(t // tm,),
        in_specs=[pl.BlockSpec((tm, d), lambda i: (i, 0)),
                  pl.BlockSpec((1, d), lambda i: (0, 0)),
                  pl.BlockSpec((1, 3, d), lambda i: ((i * tm) // seq, 0, 0))],
        out_specs=pl.BlockSpec((tm, d), lambda i: (i, 0)),
        out_shape=jax.ShapeDtypeStruct((t, d), BF16),
        compiler_params=_params("arbitrary"),
        name="norm_mod",
    )(x2, g.reshape(1, d), mod3)


def _mm_kernel(a_ref, w_ref, o_ref, *, act):
    acc = jnp.dot(a_ref[...], w_ref[...], preferred_element_type=F32)
    if act == "sigmoid":
        acc = jax.nn.sigmoid(acc)
    o_ref[...] = acc.astype(o_ref.dtype)


def _matmul(a, w, *, act=None, out_dtype=BF16, tm=1024, tn=1024, name="matmul"):
    m, k = a.shape
    n = w.shape[1]
    return pl.pallas_call(
        functools.partial(_mm_kernel, act=act),
        grid=(n // tn, m // tm),
        in_specs=[pl.BlockSpec((tm, k), lambda j, i: (i, 0)),
                  pl.BlockSpec((k, tn), lambda j, i: (0, j))],
        out_specs=pl.BlockSpec((tm, tn), lambda j, i: (i, j)),
        out_shape=jax.ShapeDtypeStruct((m, n), out_dtype),
        compiler_params=_params("arbitrary", "arbitrary"),
        name=name,
    )(a, w)


def _small_proj_kernel(a_ref, w_ref, g_ref, ckv_ref, kidx_ref, widx_ref):
    acc = jnp.dot(a_ref[...], w_ref[...], preferred_element_type=F32)
    c = acc[:, :KV_LATENT]
    cn = c * lax.rsqrt(jnp.mean(c * c, axis=-1, keepdims=True) + EPS) * g_ref[...]
    ckv_ref[...] = cn.astype(ckv_ref.dtype)
    kidx_ref[...] = acc[:, KV_LATENT:KV_LATENT + IDX_DIM].astype(kidx_ref.dtype)
    widx_ref[...] = acc[:, KV_LATENT + IDX_DIM:] * IDX_W_SCALE


def _small_proj(h, w_small, kv_g):
    t, d = h.shape
    n = w_small.shape[1]
    tm = 1024
    return pl.pallas_call(
        _small_proj_kernel,
        grid=(t // tm,),
        in_specs=[pl.BlockSpec((tm, d), lambda i: (i, 0)),
                  pl.BlockSpec((d, n), lambda i: (0, 0)),
                  pl.BlockSpec((1, KV_LATENT), lambda i: (0, 0))],
        out_specs=[pl.BlockSpec((tm, KV_LATENT), lambda i: (i, 0)),
                   pl.BlockSpec((tm, IDX_DIM), lambda i: (i, 0)),
                   pl.BlockSpec((tm, LANE), lambda i: (i, 0))],
        out_shape=[jax.ShapeDtypeStruct((t, KV_LATENT), BF16),
                   jax.ShapeDtypeStruct((t, IDX_DIM), BF16),
                   jax.ShapeDtypeStruct((t, LANE), F32)],
        compiler_params=_params("arbitrary"),
        name="small_proj",
    )(h, w_small, kv_g.reshape(1, KV_LATENT))


def _indexer_kernel(q_ref, k_ref, w_ref, bias_ref, key_scr, keyt_scr, ngt_scr, run_scr,
                    *, n_kt, k_sel):
    i = pl.program_id(1)
    q = q_ref[...]
    qs = jnp.concatenate([q[:, h * IDX_DIM:(h + 1) * IDX_DIM] for h in range(N_HEADS)], axis=0)
    w = w_ref[...]
    wcols = [w[:, h:h + 1] for h in range(N_HEADS)]
    row = lax.broadcasted_iota(I32, (Q_BLOCK, KEY_TILE), 0)
    col = lax.broadcasted_iota(I32, (Q_BLOCK, KEY_TILE), 1)
    q_chunk = (i * Q_BLOCK + row) // CHUNK
    n_used = (i * Q_BLOCK) // KEY_TILE + 1

    for j in range(n_kt):
        @pl.when(j < n_used)
        def _():
            k = k_ref[0, j * KEY_TILE:(j + 1) * KEY_TILE, :]
            logits = lax.dot_general(qs, k, (((1,), (1,)), ((), ())),
                                     preferred_element_type=F32)
            sc = wcols[0] * jnp.maximum(logits[0:Q_BLOCK], 0.0)
            for h in range(1, N_HEADS):
                sc = sc + wcols[h] * jnp.maximum(logits[h * Q_BLOCK:(h + 1) * Q_BLOCK], 0.0)
            bits = lax.bitcast_convert_type(sc, I32)
            key = bits ^ ((bits >> 31) & 0x7FFFFFFF)
            allowed = (j * KEY_TILE + col) // CHUNK <= q_chunk
            key = jnp.where(allowed, key, INT_MIN)
            key_scr[j] = key
            keyt_scr[j] = key.T

    def bisect(it, t_u):
        cand_u = t_u | jnp.left_shift(jnp.int32(1), 31 - it)
        cand_s = cand_u ^ INT_MIN

        def count(j, cnt):
            return cnt + jnp.where(keyt_scr[j] >= cand_s, 1.0, 0.0)

        cnt = lax.fori_loop(0, n_used, count, jnp.zeros((KEY_TILE, Q_BLOCK), F32))
        tot = jnp.sum(cnt, axis=0, keepdims=True)
        return jnp.where(tot >= float(k_sel), cand_u, t_u)

    t_u = lax.fori_loop(0, 32, bisect, jnp.zeros((1, Q_BLOCK), I32))
    t_s = jnp.maximum(t_u ^ INT_MIN, INT_MIN + 1)
    t_col = jnp.broadcast_to(t_s, (Q_BLOCK, Q_BLOCK)).T
    t_tile = jnp.concatenate([t_col] * (KEY_TILE // Q_BLOCK), axis=1)

    def count_ge(j, cnt):
        return cnt + jnp.where(keyt_scr[j] >= t_s, 1.0, 0.0)

    n_ge = jnp.sum(lax.fori_loop(0, n_used, count_ge, jnp.zeros((KEY_TILE, Q_BLOCK), F32)),
                   axis=0, keepdims=True)
    has_ties = jnp.max(n_ge) > float(k_sel)

    def widen(x):
        return jnp.concatenate([x] * (KEY_TILE // LANE), axis=1)

    @pl.when(jnp.logical_not(has_ties))
    def _():
        for j in range(n_kt):
            @pl.when(j < n_used)
            def _():
                bias_ref[0, 0, j] = jnp.where(key_scr[j] >= t_tile, 0.0,
                                              NEG_BIAS).astype(bias_ref.dtype)

    @pl.when(has_ties)
    def _():
        ngt_scr[...] = jnp.zeros(ngt_scr.shape, F32)
        run_scr[...] = jnp.zeros(run_scr.shape, F32)
        for j in range(n_kt):
            @pl.when(j < n_used)
            def _():
                ngt_scr[...] += jnp.sum(jnp.where(key_scr[j] > t_tile, 1.0, 0.0),
                                        axis=-1, keepdims=True)
        a = lax.broadcasted_iota(I32, (KEY_TILE, KEY_TILE), 0)
        b = lax.broadcasted_iota(I32, (KEY_TILE, KEY_TILE), 1)
        before = jnp.where(a < b, 1.0, 0.0).astype(BF16)
        room = widen(float(k_sel) - ngt_scr[...])
        for j in range(n_kt):
            @pl.when(j < n_used)
            def _():
                key = key_scr[j]
                tie = jnp.where(key == t_tile, 1.0, 0.0)
                ahead = jnp.dot(tie.astype(BF16), before, preferred_element_type=F32) \
                    + widen(run_scr[...])
                keep_tie = jnp.where(ahead < room, tie, 0.0)
                keep = jnp.where(key > t_tile, 1.0, keep_tie)
                bias_ref[0, 0, j] = jnp.where(keep > 0.5, 0.0, NEG_BIAS).astype(bias_ref.dtype)
                run_scr[...] += jnp.sum(tie, axis=-1, keepdims=True)

    for j in range(n_kt):
        @pl.when(j >= n_used)
        def _():
            bias_ref[0, 0, j] = jnp.full((Q_BLOCK, KEY_TILE), NEG_BIAS, bias_ref.dtype)


def _indexer(q_idx, k_idx3, w_idx, bsz, seq):
    n_q = seq // Q_BLOCK
    n_kt = seq // KEY_TILE
    k_sel = min(TOPK_KEYS_MAX, seq // 4)
    return pl.pallas_call(
        functools.partial(_indexer_kernel, n_kt=n_kt, k_sel=k_sel),
        grid=(bsz, n_q),
        in_specs=[pl.BlockSpec((Q_BLOCK, N_HEADS * IDX_DIM), lambda b, i: (b * n_q + i, 0)),
                  pl.BlockSpec((1, seq, IDX_DIM), lambda b, i: (b, 0, 0)),
                  pl.BlockSpec((Q_BLOCK, LANE), lambda b, i: (b * n_q + i, 0))],
        out_specs=pl.BlockSpec((1, 1, n_kt, Q_BLOCK, KEY_TILE), lambda b, i: (b, i, 0, 0, 0)),
        out_shape=jax.ShapeDtypeStruct((bsz, n_q, n_kt, Q_BLOCK, KEY_TILE), BF16),
        scratch_shapes=[pltpu.VMEM((n_kt, Q_BLOCK, KEY_TILE), I32),
                        pltpu.VMEM((n_kt, KEY_TILE, Q_BLOCK), I32),
                        pltpu.VMEM((Q_BLOCK, LANE), F32),
                        pltpu.VMEM((Q_BLOCK, LANE), F32)],
        compiler_params=_params("arbitrary", "arbitrary"),
        name="indexer",
    )(q_idx, k_idx3, w_idx)


def _attn_kernel(qa_ref, c_ref, bias_ref, wuk_ref, wuv_ref, o_ref,
                 s_scr, q_scr, m_scr, l_scr, acc_scr):
    i = pl.program_id(1)
    n_used = (i * Q_BLOCK) // KEY_TILE + 1
    rows = N_HEADS * Q_BLOCK

    qa = qa_ref[...]
    for h in range(N_HEADS):
        ql = jnp.dot(qa[:, h * HEAD_DIM:(h + 1) * HEAD_DIM], wuk_ref[h],
                     preferred_element_type=F32)
        q_scr[h * Q_BLOCK:(h + 1) * Q_BLOCK, :] = (ql * (ATTN_SCALE * LOG2_E)).astype(BF16)

    m_scr[...] = jnp.full((rows, LANE), -jnp.inf, F32)

    def scores(j, carry):
        off = pl.multiple_of(j * KEY_TILE, KEY_TILE)
        k = c_ref[0, pl.ds(off, KEY_TILE), :]
        s = lax.dot_general(q_scr[...], k, (((1,), (1,)), ((), ())),
                            preferred_element_type=F32)
        b = bias_ref[0, 0, j].astype(F32)
        for h in range(N_HEADS):
            sl = slice(h * Q_BLOCK, (h + 1) * Q_BLOCK)
            sb = s[sl] + b
            s_scr[j, sl, :] = sb
            m_scr[sl, :] = jnp.maximum(m_scr[sl, :], jnp.maximum(sb[:, :LANE], sb[:, LANE:]))
        return carry

    lax.fori_loop(0, n_used, scores, 0)

    m = jnp.max(m_scr[...], axis=-1, keepdims=True)
    l_scr[...] = jnp.zeros((rows, LANE), F32)
    acc_scr[...] = jnp.zeros((rows, KV_LATENT), F32)

    def values(j, carry):
        off = pl.multiple_of(j * KEY_TILE, KEY_TILE)
        p = jnp.exp2(s_scr[j] - m)
        l_scr[...] += p[:, :LANE] + p[:, LANE:]
        acc_scr[...] += jnp.dot(p.astype(BF16), c_ref[0, pl.ds(off, KEY_TILE), :],
                                preferred_element_type=F32)
        return carry

    lax.fori_loop(0, n_used, values, 0)

    inv_l = 1.0 / jnp.sum(l_scr[...], axis=-1, keepdims=True)
    o = (acc_scr[...] * inv_l).astype(BF16)
    for h in range(N_HEADS):
        o_ref[:, h * HEAD_DIM:(h + 1) * HEAD_DIM] = jnp.dot(
            o[h * Q_BLOCK:(h + 1) * Q_BLOCK], wuv_ref[h],
            preferred_element_type=F32).astype(o_ref.dtype)


def _attention(q_a, c_kv3, bias, w_uk, w_uv, bsz, seq):
    n_q = seq // Q_BLOCK
    n_kt = seq // KEY_TILE
    rows = N_HEADS * Q_BLOCK
    t = bsz * seq
    return pl.pallas_call(
        _attn_kernel,
        grid=(bsz, n_q),
        in_specs=[pl.BlockSpec((Q_BLOCK, N_HEADS * HEAD_DIM), lambda b, i: (b * n_q + i, 0)),
                  pl.BlockSpec((1, seq, KV_LATENT), lambda b, i: (b, 0, 0)),
                  pl.BlockSpec((1, 1, n_kt, Q_BLOCK, KEY_TILE), lambda b, i: (b, i, 0, 0, 0)),
                  pl.BlockSpec((N_HEADS, HEAD_DIM, KV_LATENT), lambda b, i: (0, 0, 0)),
                  pl.BlockSpec((N_HEADS, KV_LATENT, HEAD_DIM), lambda b, i: (0, 0, 0))],
        out_specs=pl.BlockSpec((Q_BLOCK, N_HEADS * HEAD_DIM), lambda b, i: (b * n_q + i, 0)),
        out_shape=jax.ShapeDtypeStruct((t, N_HEADS * HEAD_DIM), BF16),
        scratch_shapes=[pltpu.VMEM((n_kt, rows, KEY_TILE), F32),
                        pltpu.VMEM((rows, KV_LATENT), BF16),
                        pltpu.VMEM((rows, LANE), F32),
                        pltpu.VMEM((rows, LANE), F32),
                        pltpu.VMEM((rows, KV_LATENT), F32)],
        compiler_params=_params("arbitrary", "arbitrary"),
        name="attention",
    )(q_a, c_kv3, bias, w_uk, w_uv)


def _spatial_kernel(z_ref, g_ref, ws_ref, bs_ref, o_ref, *, width):
    z = z_ref[...].astype(F32)
    z = 0.5 * z * (1.0 + lax.erf(z * (2.0 ** -0.5)))
    u = z[:, :width]
    v = z[:, width:]
    mu = jnp.mean(v, axis=-1, keepdims=True)
    vc = v - mu
    var = jnp.mean(vc * vc, axis=-1, keepdims=True)
    vn = (vc * lax.rsqrt(var + EPS) * g_ref[...]).astype(BF16)
    r = lax.broadcasted_iota(I32, (SG_CHUNK, SG_CHUNK), 0)
    c = lax.broadcasted_iota(I32, (SG_CHUNK, SG_CHUNK), 1)
    gd = width // N_GROUPS_B
    bs = bs_ref[...]
    for g in range(N_GROUPS_B):
        wg = jnp.where(r >= c, ws_ref[g], 0.0).astype(BF16)
        s = jnp.dot(wg, vn[:, g * gd:(g + 1) * gd], preferred_element_type=F32) + bs[:, g:g + 1]
        o_ref[:, g * gd:(g + 1) * gd] = (u[:, g * gd:(g + 1) * gd] * s).astype(o_ref.dtype)


def _spatial_gating(z, sg_g, w_spatial, b_spatial_t):
    t, w2 = z.shape
    width = w2 // 2
    return pl.pallas_call(
        functools.partial(_spatial_kernel, width=width),
        grid=(t // SG_CHUNK,),
        in_specs=[pl.BlockSpec((SG_CHUNK, w2), lambda i: (i, 0)),
                  pl.BlockSpec((1, width), lambda i: (0, 0)),
                  pl.BlockSpec((N_GROUPS_B, SG_CHUNK, SG_CHUNK), lambda i: (0, 0, 0)),
                  pl.BlockSpec((SG_CHUNK, LANE), lambda i: (0, 0))],
        out_specs=pl.BlockSpec((SG_CHUNK, width), lambda i: (i, 0)),
        out_shape=jax.ShapeDtypeStruct((t, width), BF16),
        compiler_params=_params("arbitrary"),
        name="spatial_gating",
    )(z, sg_g.reshape(1, width), w_spatial, b_spatial_t)


def _mix_kernel(a_ref, b_ref, wa_ref, wb_ref, ga_ref, gb_ref, o_ref):
    ya = jnp.dot(a_ref[...], wa_ref[...], preferred_element_type=F32)
    yb = jnp.dot(b_ref[...], wb_ref[...], preferred_element_type=F32)
    o_ref[...] = (ga_ref[...].astype(F32) * ya + gb_ref[...].astype(F32) * yb).astype(o_ref.dtype)


def _mix(o_a, sg, w_a, w_b, gates):
    t, k = o_a.shape
    d = w_a.shape[1]
    tm, tn = 1024, 1024
    nb = d // tn
    return pl.pallas_call(
        _mix_kernel,
        grid=(nb, t // tm),
        in_specs=[pl.BlockSpec((tm, k), lambda j, i: (i, 0)),
                  pl.BlockSpec((tm, k), lambda j, i: (i, 0)),
                  pl.BlockSpec((k, tn), lambda j, i: (0, j)),
                  pl.BlockSpec((k, tn), lambda j, i: (0, j)),
                  pl.BlockSpec((tm, tn), lambda j, i: (i, j)),
                  pl.BlockSpec((tm, tn), lambda j, i: (i, nb + j))],
        out_specs=pl.BlockSpec((tm, tn), lambda j, i: (i, j)),
        out_shape=jax.ShapeDtypeStruct((t, d), BF16),
        compiler_params=_params("arbitrary", "arbitrary"),
        name="branch_mix",
    )(o_a, sg, w_a, w_b, gates, gates)


def _out_router_kernel(a_ref, w_ref, x_ref, mod_ref, g_ref, wr_ref, br_ref,
                       x1_ref, h2_ref, lg_ref):
    mix = jnp.dot(a_ref[...], w_ref[...], preferred_element_type=F32)
    x1 = x_ref[...] + mod_ref[0, 2:3, :] * mix
    x1_ref[...] = x1
    y = x1 * lax.rsqrt(jnp.mean(x1 * x1, axis=-1, keepdims=True) + EPS) * g_ref[...]
    h2 = y * (1.0 + mod_ref[0, 4:5, :]) + mod_ref[0, 3:4, :]
    h2_ref[...] = h2
    h_hi = h2.astype(BF16)
    h_lo = (h2 - h_hi.astype(F32)).astype(BF16)
    wr = wr_ref[...]
    w_hi = wr.astype(BF16)
    w_lo = (wr - w_hi.astype(F32)).astype(BF16)
    lg = jnp.dot(h_hi, w_hi, preferred_element_type=F32)
    lg = lg + jnp.dot(h_lo, w_hi, preferred_element_type=F32)
    lg = lg + jnp.dot(h_hi, w_lo, preferred_element_type=F32)
    lg_ref[...] = lg + br_ref[...]


def _out_router(mixpre, w_out, x2, mod6, norm2_g, w_router_p, b_router_p, seq):
    t, d = x2.shape
    tm = 256
    return pl.pallas_call(
        _out_router_kernel,
        grid=(t // tm,),
        in_specs=[pl.BlockSpec((tm, d), lambda i: (i, 0)),
                  pl.BlockSpec((d, d), lambda i: (0, 0)),
                  pl.BlockSpec((tm, d), lambda i: (i, 0)),
                  pl.BlockSpec((1, 6, d), lambda i: ((i * tm) // seq, 0, 0)),
                  pl.BlockSpec((1, d), lambda i: (0, 0)),
                  pl.BlockSpec((d, LANE), lambda i: (0, 0)),
                  pl.BlockSpec((1, LANE), lambda i: (0, 0))],
        out_specs=[pl.BlockSpec((tm, d), lambda i: (i, 0)),
                   pl.BlockSpec((tm, d), lambda i: (i, 0)),
                   pl.BlockSpec((tm, LANE), lambda i: (i, 0))],
        out_shape=[jax.ShapeDtypeStruct((t, d), F32),
                   jax.ShapeDtypeStruct((t, d), F32),
                   jax.ShapeDtypeStruct((t, LANE), F32)],
        compiler_params=_params("arbitrary"),
        name="out_proj_router",
    )(mixpre, w_out, x2, mod6, norm2_g.reshape(1, d), w_router_p, b_router_p)


DISPATCH_TOKENS = 512


def _pack_bf16_pairs(x):
    half = x.shape[1] // 2
    hi = lax.bitcast_convert_type(x[:, :half].astype(BF16).astype(F32), I32)
    lo = lax.bitcast_convert_type(x[:, half:].astype(BF16).astype(F32), I32)
    return hi | lax.shift_right_logical(lo, 16)


def _unpack_bf16_pairs(p):
    hi = lax.bitcast_convert_type(p & jnp.int32(-65536), F32).astype(BF16)
    lo = lax.bitcast_convert_type(lax.shift_left(p, 16), F32).astype(BF16)
    return hi, lo


def _dispatch_kernel(pos_ref, pad_ref, h_ref, xs_hbm, pk_scr, zero_scr, sem, *, n_alloc_blocks):
    i = pl.program_id(0)
    n_tiles = pl.num_programs(0)
    packed = _pack_bf16_pairs(h_ref[...])
    for c in range(pk_scr.shape[1]):
        pk_scr[:, c, :] = packed[:, c * LANE:(c + 1) * LANE]
    zero_scr[...] = jnp.zeros(zero_scr.shape, zero_scr.dtype)

    def row_copy(t, k):
        return pltpu.make_async_copy(
            pk_scr.at[t], xs_hbm.at[pos_ref[0, 0, t * TOP_K + k]], sem)

    def pad_copy(r):
        return pltpu.make_async_copy(zero_scr.at[0], xs_hbm.at[r], sem)

    def tail_copy(b):
        off = pl.multiple_of(b * MOE_ROWS, MOE_ROWS)
        return pltpu.make_async_copy(zero_scr, xs_hbm.at[pl.ds(off, MOE_ROWS)], sem)

    e_lo = (i * N_EXPERTS) // n_tiles
    e_hi = ((i + 1) * N_EXPERTS) // n_tiles
    tail_lo = jnp.where(i == 0, pad_ref[2, 0], n_alloc_blocks)

    def for_rows(fn):
        def per_token(t, carry):
            for k in range(TOP_K):
                fn(row_copy(t, k))
            return carry

        def per_expert(e, carry):
            def per_pad(r, c):
                fn(pad_copy(r))
                return c
            return lax.fori_loop(pad_ref[0, e], pad_ref[1, e], per_pad, carry)

        def per_tail(b, carry):
            fn(tail_copy(b))
            return carry

        lax.fori_loop(0, DISPATCH_TOKENS, per_token, 0)
        lax.fori_loop(e_lo, e_hi, per_expert, 0)
        lax.fori_loop(tail_lo, n_alloc_blocks, per_tail, 0)

    for_rows(lambda cp: cp.start())
    for_rows(lambda cp: cp.wait())


def _dispatch(pos, pad_rng, h2, n_alloc_rows):
    t, d = h2.shape
    n_tiles = t // DISPATCH_TOKENS
    pos3 = pos.reshape(n_tiles, 1, DISPATCH_TOKENS * TOP_K)
    return pl.pallas_call(
        functools.partial(_dispatch_kernel, n_alloc_blocks=n_alloc_rows // MOE_ROWS),
        grid=(n_tiles,),
        in_specs=[pl.BlockSpec((1, 1, DISPATCH_TOKENS * TOP_K), lambda i: (i, 0, 0),
                               memory_space=pltpu.SMEM),
                  pl.BlockSpec(memory_space=pltpu.SMEM),
                  pl.BlockSpec((DISPATCH_TOKENS, d), lambda i: (i, 0))],
        out_specs=pl.BlockSpec(memory_space=pl.ANY),
        out_shape=jax.ShapeDtypeStruct((n_alloc_rows, d // (2 * LANE), LANE), I32),
        scratch_shapes=[pltpu.VMEM((DISPATCH_TOKENS, d // (2 * LANE), LANE), I32),
                        pltpu.VMEM((MOE_ROWS, d // (2 * LANE), LANE), I32),
                        pltpu.SemaphoreType.DMA(())],
        compiler_params=_params("arbitrary"),
        name="moe_dispatch",
    )(pos3, pad_rng, h2)


def _ffn_kernel(ie_ref, in_ref, ix_ref, io_ref, no_ref, xs_hbm, wg_ref, wl_ref, bg_ref, bl_ref,
                wd_ref, bd_ref, o_hbm, x_scr, xw_scr, acc_scr, sem, sem_x):
    del ie_ref
    i = pl.program_id(0)
    f = pl.program_id(1)
    n_items = pl.num_programs(0)
    n_f = pl.num_programs(1)
    n_sub = in_ref[i]
    half = x_scr.shape[1] // 2
    r = MOE_ROWS
    win = xw_scr.shape[1]
    o_ref = acc_scr.at[i % 2]

    def for_window(item, fn):
        off = pl.multiple_of(ix_ref[item] * r, r)
        for c in range(half // LANE):
            fn(pltpu.make_async_copy(xs_hbm.at[pl.ds(off, win), c, :],
                                     xw_scr.at[item % 2, :, pl.ds(c * LANE, LANE)],
                                     sem_x.at[item % 2]))

    def out_copy(item, s):
        off = pl.multiple_of(io_ref[item] * r, r)
        return pltpu.make_async_copy(acc_scr.at[item % 2, pl.ds(0, s * r), pl.ds(0, half)],
                                     o_hbm.at[pl.ds(off, s * r), :], sem.at[item % 2])

    def for_out(item, fn):
        for s in range(1, ITEM_SUB + 1):
            @pl.when(no_ref[item] == s)
            def _():
                fn(out_copy(item, s))

    @pl.when(f == 0)
    def _():
        @pl.when(i == 0)
        def _():
            for_window(0, lambda cp: cp.start())

        @pl.when(i + 1 < n_items)
        def _():
            for_window(i + 1, lambda cp: cp.start())

        for_window(i, lambda cp: cp.wait())

        @pl.when(i >= 2)
        def _():
            for_out(i - 2, lambda cp: cp.wait())
        o_ref[...] = jnp.broadcast_to(bd_ref[0], o_ref.shape)

    def unpack(rows):
        hi, lo = _unpack_bf16_pairs(xw_scr[i % 2, :rows, :])
        x_scr[:rows, :half] = hi
        x_scr[:rows, half:] = lo

    def ffn(xb):
        g = jnp.dot(xb, wg_ref[0].astype(BF16), preferred_element_type=F32) + bg_ref[0]
        lin = jnp.dot(xb, wl_ref[0].astype(BF16), preferred_element_type=F32) + bl_ref[0]
        g = jnp.minimum(g, SWIGLU_LIMIT)
        lin = jnp.clip(lin, -SWIGLU_LIMIT, SWIGLU_LIMIT)
        act = g * jax.nn.sigmoid(SWIGLU_ALPHA * g) * (lin + 1.0)
        return jnp.dot(act.astype(BF16), wd_ref[0].astype(BF16), preferred_element_type=F32)

    for s in range(1, ITEM_SUB + 1):
        rows = s * r

        @pl.when(jnp.logical_and(n_sub == s, f == 0))
        def _():
            unpack(rows)

        @pl.when(n_sub == s)
        def _():
            o_ref[:rows, :] += ffn(x_scr[:rows, :])

    @pl.when(f == n_f - 1)
    def _():
        o_ref[:, :half] = lax.bitcast_convert_type(_pack_bf16_pairs(o_ref[...]), F32)
        for_out(i, lambda cp: cp.start())

        @pl.when(i == n_items - 1)
        def _():
            @pl.when(i >= 1)
            def _():
                for_out(i - 1, lambda cp: cp.wait())
            for_out(i, lambda cp: cp.wait())


def _expert_ffn(items, xs, w_gate_up, b_gate_up3, w_down, b_down3):
    p = xs.shape[0]
    dh = xs.shape[1] * xs.shape[2]
    d = 2 * dh
    n_e, _, f2 = w_gate_up.shape
    d_ff = f2 // 2
    tf = FFN_TILE
    nf = d_ff // tf
    n_items = items[0].shape[0]
    win = ITEM_SUB * MOE_ROWS

    def ft(f, ni, i):
        return jnp.where(ni[i] > 0, f, nf - 1)

    grid_spec = pltpu.PrefetchScalarGridSpec(
        num_scalar_prefetch=5,
        grid=(n_items, nf),
        in_specs=[pl.BlockSpec(memory_space=pl.ANY),
                  pl.BlockSpec((1, d, tf), lambda i, f, ie, ni, ix, io, no: (ie[i], 0, ft(f, ni, i))),
                  pl.BlockSpec((1, d, tf),
                               lambda i, f, ie, ni, ix, io, no: (ie[i], 0, nf + ft(f, ni, i))),
                  pl.BlockSpec((1, 1, tf), lambda i, f, ie, ni, ix, io, no: (ie[i], 0, ft(f, ni, i))),
                  pl.BlockSpec((1, 1, tf),
                               lambda i, f, ie, ni, ix, io, no: (ie[i], 0, nf + ft(f, ni, i))),
                  pl.BlockSpec((1, tf, d), lambda i, f, ie, ni, ix, io, no: (ie[i], ft(f, ni, i), 0)),
                  pl.BlockSpec((1, 1, d), lambda i, f, ie, ni, ix, io, no: (ie[i], 0, 0))],
        out_specs=pl.BlockSpec(memory_space=pl.ANY),
        scratch_shapes=[pltpu.VMEM((win, d), BF16),
                        pltpu.VMEM((2, win, dh), I32),
                        pltpu.VMEM((2, win, d), F32),
                        pltpu.SemaphoreType.DMA((2,)),
                        pltpu.SemaphoreType.DMA((2,))],
    )
    return pl.pallas_call(
        _ffn_kernel,
        grid_spec=grid_spec,
        out_shape=jax.ShapeDtypeStruct((p, dh), F32),
        compiler_params=_params("arbitrary", "arbitrary"),
        name="expert_ffn",
    )(*items, xs, w_gate_up, w_gate_up, b_gate_up3, b_gate_up3, w_down, b_down3)


COMBINE_TOKENS = 256


def _combine_kernel(pos_ref, pos_next_ref, ys_hbm, gate_ref, x1_ref, mod_ref, g_ref, o_ref,
                    buf, sem):
    i = pl.program_id(0)
    n = pl.num_programs(0)
    slot = i % 2

    def issue(p_ref, sl):
        def body(t, carry):
            for k in range(TOP_K):
                pltpu.make_async_copy(ys_hbm.at[pl.ds(p_ref[0, 0, t * TOP_K + k], 1), :],
                                      buf.at[sl, k, pl.ds(t, 1), :], sem.at[sl]).start()
            return carry
        lax.fori_loop(0, COMBINE_TOKENS, body, 0)

    def wait_all(sl):
        def body(t, carry):
            for k in range(TOP_K):
                pltpu.make_async_copy(ys_hbm.at[pl.ds(0, 1), :],
                                      buf.at[sl, k, pl.ds(t, 1), :], sem.at[sl]).wait()
            return carry
        lax.fori_loop(0, COMBINE_TOKENS, body, 0)

    @pl.when(i == 0)
    def _():
        issue(pos_ref, 0)

    @pl.when(i + 1 < n)
    def _():
        issue(pos_next_ref, 1 - slot)

    wait_all(slot)

    gate = gate_ref[...]
    d = x1_ref.shape[1]
    half = d // 2
    moe_hi = jnp.zeros((COMBINE_TOKENS, half), F32)
    moe_lo = jnp.zeros((COMBINE_TOKENS, half), F32)
    for k in range(TOP_K):
        bits = lax.bitcast_convert_type(buf[slot, k], I32)
        hi = lax.bitcast_convert_type(bits & jnp.int32(-65536), F32)
        lo = lax.bitcast_convert_type(lax.shift_left(bits, 16), F32)
        moe_hi = moe_hi + gate[:, k:k + 1] * hi
        moe_lo = moe_lo + gate[:, k:k + 1] * lo
    x_hi = x1_ref[:, :half] + mod_ref[0, 5:6, :half] * moe_hi
    x_lo = x1_ref[:, half:] + mod_ref[0, 5:6, half:] * moe_lo
    ss = jnp.sum(x_hi * x_hi, axis=-1, keepdims=True) + jnp.sum(x_lo * x_lo, axis=-1, keepdims=True)
    inv = lax.rsqrt(ss * (1.0 / d) + EPS)
    o_ref[:, :half] = x_hi * inv * g_ref[:, :half]
    o_ref[:, half:] = x_lo * inv * g_ref[:, half:]


def _combine(pos, ys, gates_p, x1, mod6, final_g, seq):
    t, d = x1.shape
    tt = COMBINE_TOKENS
    n_tiles = t // tt
    pos3 = pos.reshape(n_tiles, 1, tt * TOP_K)
    return pl.pallas_call(
        _combine_kernel,
        grid=(n_tiles,),
        in_specs=[pl.BlockSpec((1, 1, tt * TOP_K), lambda i: (i, 0, 0), memory_space=pltpu.SMEM),
                  pl.BlockSpec((1, 1, tt * TOP_K), lambda i: (jnp.minimum(i + 1, n_tiles - 1), 0, 0),
                               memory_space=pltpu.SMEM),
                  pl.BlockSpec(memory_space=pl.ANY),
                  pl.BlockSpec((tt, LANE), lambda i: (i, 0)),
                  pl.BlockSpec((tt, d), lambda i: (i, 0)),
                  pl.BlockSpec((1, 6, d), lambda i: ((i * tt) // seq, 0, 0)),
                  pl.BlockSpec((1, d), lambda i: (0, 0))],
        out_specs=pl.BlockSpec((tt, d), lambda i: (i, 0)),
        out_shape=jax.ShapeDtypeStruct((t, d), F32),
        scratch_shapes=[pltpu.VMEM((2, TOP_K, tt, d // 2), F32),
                        pltpu.SemaphoreType.DMA((2,))],
        compiler_params=_params("arbitrary"),
        name="moe_combine",
    )(pos3, pos3, ys, gates_p, x1, mod6, final_g.reshape(1, d))


ROUTE_TOKENS = 512


def _route_kernel(lg_ref, idx_ref, gate_ref, rank_ref, cnt_ref, carry_scr):
    i = pl.program_id(0)
    tt = lg_ref.shape[0]

    @pl.when(i == 0)
    def _():
        carry_scr[...] = jnp.zeros(carry_scr.shape, F32)

    lane = lax.broadcasted_iota(I32, (tt, LANE), 1)
    v = jnp.where(lane < N_EXPERTS, lg_ref[...], -jnp.inf)
    tops, ids, hits = [], [], []
    for _ in range(TOP_K):
        m = jnp.max(v, axis=-1, keepdims=True)
        idx = jnp.min(jnp.where(v == m, lane, LANE), axis=-1, keepdims=True)
        hit = lane == idx
        v = jnp.where(hit, -jnp.inf, v)
        tops.append(m)
        ids.append(idx)
        hits.append(hit)

    exps = [jnp.exp(m - tops[0]) for m in tops]
    total = exps[0]
    for e in exps[1:]:
        total = total + e

    sel = jnp.where(hits[0], 1.0, 0.0)
    for hit in hits[1:]:
        sel = sel + jnp.where(hit, 1.0, 0.0)
    r = lax.broadcasted_iota(I32, (tt, tt), 0)
    c = lax.broadcasted_iota(I32, (tt, tt), 1)
    earlier = jnp.where(c < r, 1.0, 0.0).astype(BF16)
    rank_all = jnp.dot(earlier, sel.astype(BF16), preferred_element_type=F32) + carry_scr[...]
    carry_scr[...] += jnp.sum(sel, axis=0, keepdims=True)
    cnt_ref[...] = carry_scr[...]

    idx_out = jnp.zeros((tt, LANE), I32)
    gate_out = jnp.zeros((tt, LANE), F32)
    rank_out = jnp.zeros((tt, LANE), F32)
    for k in range(TOP_K):
        rk = jnp.sum(jnp.where(hits[k], rank_all, 0.0), axis=-1, keepdims=True)
        idx_out = jnp.where(lane == k, ids[k], idx_out)
        gate_out = jnp.where(lane == k, exps[k] / total, gate_out)
        rank_out = jnp.where(lane == k, rk, rank_out)
    idx_ref[...] = idx_out
    gate_ref[...] = gate_out
    rank_ref[...] = rank_out.astype(I32)


def _route(logits_p):
    t = logits_p.shape[0]
    tt = ROUTE_TOKENS
    return pl.pallas_call(
        _route_kernel,
        grid=(t // tt,),
        in_specs=[pl.BlockSpec((tt, LANE), lambda i: (i, 0))],
        out_specs=[pl.BlockSpec((tt, LANE), lambda i: (i, 0)),
                   pl.BlockSpec((tt, LANE), lambda i: (i, 0)),
                   pl.BlockSpec((tt, LANE), lambda i: (i, 0)),
                   pl.BlockSpec((1, LANE), lambda i: (0, 0))],
        out_shape=[jax.ShapeDtypeStruct((t, LANE), I32),
                   jax.ShapeDtypeStruct((t, LANE), F32),
                   jax.ShapeDtypeStruct((t, LANE), I32),
                   jax.ShapeDtypeStruct((1, LANE), F32)],
        scratch_shapes=[pltpu.VMEM((1, LANE), F32)],
        compiler_params=_params("arbitrary"),
        name="moe_route",
    )(logits_p)


def _routing(logits_p):
    t = logits_p.shape[0]
    idx_p, gates_p, rank_p, cnt = _route(logits_p)
    top_idx = idx_p[:, :TOP_K]
    counts = cnt[0, :N_EXPERTS].astype(I32)
    padded = ((counts + MOE_ROWS - 1) // MOE_ROWS) * MOE_ROWS
    pad_ends = jnp.cumsum(padded)
    pstarts = pad_ends - padded
    pos = (pstarts[top_idx] + rank_p[:, :TOP_K]).astype(I32)

    win = ITEM_SUB * MOE_ROWS
    n_rows = t * TOP_K + N_EXPERTS * MOE_ROWS
    n_alloc = n_rows + win
    n_items = (n_rows // MOE_ROWS + N_EXPERTS * (ITEM_SUB - 1)) // ITEM_SUB + 1

    tail0 = pad_ends[-1]
    pad_rng = jnp.stack([pstarts + counts, pad_ends,
                         jnp.full((N_EXPERTS,), tail0 // MOE_ROWS)]).astype(I32)

    nb_e = padded // MOE_ROWS
    items_e = (nb_e + ITEM_SUB - 1) // ITEM_SUB
    item_ends = jnp.cumsum(items_e)
    item_starts = item_ends - items_e
    n_used = item_ends[-1]
    q = jnp.arange(n_items, dtype=I32)
    e_q = jnp.clip(jnp.searchsorted(item_ends, q, side="right"), 0, N_EXPERTS - 1)
    local = q - item_starts[e_q]
    used = q < n_used
    item_n = jnp.where(used, jnp.minimum(ITEM_SUB, nb_e[e_q] - ITEM_SUB * local), 0)
    item_x = jnp.where(used, pstarts[e_q] // MOE_ROWS + local * ITEM_SUB, 0)
    n_alloc_blocks = n_alloc // MOE_ROWS
    tail_blk = tail0 // MOE_ROWS + (q - n_used) * ITEM_SUB
    item_o = jnp.where(used, item_x, jnp.minimum(tail_blk, n_alloc_blocks - 1))
    item_no = jnp.where(used, item_n, jnp.clip(n_alloc_blocks - tail_blk, 0, ITEM_SUB))
    e_last = jnp.max(jnp.where(items_e > 0, jnp.arange(N_EXPERTS, dtype=I32), 0))
    item_e = jnp.where(used, e_q, e_last)
    items = tuple(a.astype(I32) for a in (item_e, item_n, item_x, item_o, item_no))
    return pos, gates_p, pad_rng, items, n_alloc


def _layer(x2, c_act_in, bsz, seq, w_mod, b_mod, norm1_g, w_in, kv_norm_g, w_uk, w_uv,
           w_proj_a, sg_norm_g, w_spatial, b_spatial, w_proj_b, w_out, norm2_g,
           w_router, b_router, w_gate_up, b_gate_up, w_down, b_down):
    t, d = x2.shape
    w_qa = N_HEADS * HEAD_DIM
    w_qi = N_HEADS * IDX_DIM
    width_b = sg_norm_g.shape[0]

    mod = _modulation(c_act_in, w_mod, b_mod)
    mod6 = mod.reshape(bsz, 6, d)
    h = _norm_mod(x2, norm1_g, mod6[:, :3], seq)

    o0 = w_qa
    o1 = o0 + KV_LATENT
    o2 = o1 + w_qi
    o3 = o2 + IDX_DIM
    o4 = o3 + N_HEADS
    o5 = o4 + 2 * width_b
    w_small = jnp.concatenate(
        [w_in[:, o0:o1], w_in[:, o2:o3], w_in[:, o3:o4],
         jnp.zeros((d, LANE - N_HEADS), w_in.dtype)], axis=1).astype(BF16)

    q_a = _matmul(h, w_in[:, :o0].astype(BF16), name="proj_q_a")
    q_idx = _matmul(h, w_in[:, o1:o2].astype(BF16), name="proj_q_idx")
    z_b = _matmul(h, w_in[:, o4:o5].astype(BF16), name="proj_z_b")
    gates = _matmul(h, w_in[:, o5:].astype(BF16), act="sigmoid", name="proj_gates")
    c_kv, k_idx, w_idx = _small_proj(h, w_small, kv_norm_g)

    bias = _indexer(q_idx, k_idx.reshape(bsz, seq, IDX_DIM), w_idx, bsz, seq)
    o_a = _attention(q_a, c_kv.reshape(bsz, seq, KV_LATENT), bias,
                     w_uk.astype(BF16), w_uv.astype(BF16), bsz, seq)

    b_sp_t = jnp.pad(jnp.transpose(b_spatial), ((0, 0), (0, LANE - N_GROUPS_B)))
    sg = _spatial_gating(z_b, sg_norm_g, w_spatial, b_sp_t)

    mixpre = _mix(o_a, sg, w_proj_a.astype(BF16), w_proj_b.astype(BF16), gates)

    w_router_p = jnp.pad(w_router, ((0, 0), (0, LANE - N_EXPERTS)))
    b_router_p = jnp.pad(b_router, (0, LANE - N_EXPERTS)).reshape(1, LANE)
    x1, h2, logits_p = _out_router(mixpre, w_out.astype(BF16), x2, mod6, norm2_g,
                                   w_router_p, b_router_p, seq)

    pos, gates_p, pad_rng, items, n_alloc = _routing(logits_p)
    xs = _dispatch(pos.reshape(-1), pad_rng, h2, n_alloc)
    n_e = w_gate_up.shape[0]
    ys = _expert_ffn(items, xs, w_gate_up, b_gate_up.reshape(n_e, 1, -1),
                     w_down, b_down.reshape(n_e, 1, -1))
    return pos, ys, gates_p, x1, mod6


def kernel(x, c, w_mod, b_mod, norm1_g, w_in, kv_norm_g, w_uk, w_uv, w_proj_a, sg_norm_g,
           w_spatial, b_spatial, w_proj_b, w_out, norm2_g, w_router, b_router, w_gate_up,
           b_gate_up, w_down, b_down, final_g):
    bsz, seq, d = x.shape
    depth = w_mod.shape[0]
    assert depth == 1, "single-layer block"
    x2 = x.reshape(bsz * seq, d)
    pos, ys, gates_p, x1, mod6 = _layer(
        x2, c, bsz, seq, w_mod[0], b_mod[0], norm1_g[0], w_in[0], kv_norm_g[0], w_uk[0],
        w_uv[0], w_proj_a[0], sg_norm_g[0], w_spatial[0], b_spatial[0], w_proj_b[0],
        w_out[0], norm2_g[0], w_router[0], b_router[0], w_gate_up[0], b_gate_up[0],
        w_down[0], b_down[0])
    out = _combine(pos.reshape(-1), ys, gates_p, x1, mod6, final_g, seq)
    return out.reshape(bsz, seq, d)
```

```python
import functools

import jax
import jax.numpy as jnp
from jax import lax
from jax.experimental import pallas as pl
from jax.experimental.pallas import tpu as pltpu

F32 = jnp.float32
BF16 = jnp.bfloat16
I32 = jnp.int32

EPS = 1e-6
CHUNK = 64
N_HEADS = 16
HEAD_DIM = 128
KV_LATENT = 256
IDX_DIM = 128
TOPK_KEYS_MAX = 256
Q_BLOCK = 128
KEY_TILE = 256
N_GROUPS_B = 8
SG_CHUNK = 128
N_EXPERTS = 32
TOP_K = 4
SWIGLU_ALPHA = 1.702
SWIGLU_LIMIT = 7.0
ATTN_SCALE = HEAD_DIM ** -0.5
LOG2_E = 1.4426950408889634
IDX_W_SCALE = (N_HEADS ** -0.5) * (IDX_DIM ** -0.5)

LANE = 128
MOE_ROWS = 512
ITEM_SUB = 2
FFN_TILE = 256
WEIGHT_SLOTS = 3
NEG_BIAS = -1e30
INT_MIN = -2147483648
VMEM_LIMIT = 56 * 1024 * 1024


def _params(*sem):
    return pltpu.CompilerParams(dimension_semantics=sem, vmem_limit_bytes=VMEM_LIMIT)


def _mod_kernel(c_ref, w_ref, b_ref, o_ref):
    c = c_ref[...]
    ca = (c * jax.nn.sigmoid(c)).astype(BF16)
    o_ref[...] = jnp.dot(ca, w_ref[...].astype(BF16), preferred_element_type=F32) + b_ref[...]


def _modulation(c, w_mod, b_mod):
    bsz, d = c.shape
    n = w_mod.shape[1]
    tn = 1024
    return pl.pallas_call(
        _mod_kernel,
        grid=(n // tn,),
        in_specs=[pl.BlockSpec((bsz, d), lambda j: (0, 0)),
                  pl.BlockSpec((d, tn), lambda j: (0, j)),
                  pl.BlockSpec((1, tn), lambda j: (0, j))],
        out_specs=pl.BlockSpec((bsz, tn), lambda j: (0, j)),
        out_shape=jax.ShapeDtypeStruct((bsz, n), F32),
        compiler_params=_params("arbitrary"),
        name="modulation",
    )(c, w_mod, b_mod.reshape(1, n))


def _norm_mod_kernel(x_ref, g_ref, mod_ref, o_ref):
    x = x_ref[...]
    y = x * lax.rsqrt(jnp.mean(x * x, axis=-1, keepdims=True) + EPS) * g_ref[...]
    o_ref[...] = (y * (1.0 + mod_ref[0, 1:2, :]) + mod_ref[0, 0:1, :]).astype(o_ref.dtype)


def _norm_mod(x2, g, mod3, seq):
    t, d = x2.shape
    tm = 512
    return pl.pallas_call(
        _norm_mod_kernel,
        grid=(t // tm,),
        in_specs=[pl.BlockSpec((tm, d), lambda i: (i, 0)),
                  pl.BlockSpec((1, d), lambda i: (0, 0)),
                  pl.BlockSpec((1, 3, d), lambda i: ((i * tm) // seq, 0, 0))],
        out_specs=pl.BlockSpec((tm, d), lambda i: (i, 0)),
        out_shape=jax.ShapeDtypeStruct((t, d), BF16),
        compiler_params=_params("arbitrary"),
        name="norm_mod",
    )(x2, g.reshape(1, d), mod3)


def _mm_kernel(a_ref, w_ref, o_ref, *, act):
    acc = jnp.dot(a_ref[...], w_ref[...], preferred_element_type=F32)
    if act == "sigmoid":
        acc = jax.nn.sigmoid(acc)
    o_ref[...] = acc.astype(o_ref.dtype)


def _matmul(a, w, *, act=None, out_dtype=BF16, tm=1024, tn=1024, name="matmul"):
    m, k = a.shape
    n = w.shape[1]
    return pl.pallas_call(
        functools.partial(_mm_kernel, act=act),
        grid=(n // tn, m // tm),
        in_specs=[pl.BlockSpec((tm, k), lambda j, i: (i, 0)),
                  pl.BlockSpec((k, tn), lambda j, i: (0, j))],
        out_specs=pl.BlockSpec((tm, tn), lambda j, i: (i, j)),
        out_shape=jax.ShapeDtypeStruct((m, n), out_dtype),
        compiler_params=_params("arbitrary", "arbitrary"),
        name=name,
    )(a, w)


def _small_proj_kernel(a_ref, w_ref, g_ref, ckv_ref, kidx_ref, widx_ref):
    acc = jnp.dot(a_ref[...], w_ref[...], preferred_element_type=F32)
    c = acc[:, :KV_LATENT]
    cn = c * lax.rsqrt(jnp.mean(c * c, axis=-1, keepdims=True) + EPS) * g_ref[...]
    ckv_ref[...] = cn.astype(ckv_ref.dtype)
    kidx_ref[...] = acc[:, KV_LATENT:KV_LATENT + IDX_DIM].astype(kidx_ref.dtype)
    widx_ref[...] = acc[:, KV_LATENT + IDX_DIM:] * IDX_W_SCALE


def _small_proj(h, w_small, kv_g):
    t, d = h.shape
    n = w_small.shape[1]
    tm = 1024
    return pl.pallas_call(
        _small_proj_kernel,
        grid=(t // tm,),
        in_specs=[pl.BlockSpec((tm, d), lambda i: (i, 0)),
                  pl.BlockSpec((d, n), lambda i: (0, 0)),
                  pl.BlockSpec((1, KV_LATENT), lambda i: (0, 0))],
        out_specs=[pl.BlockSpec((tm, KV_LATENT), lambda i: (i, 0)),
                   pl.BlockSpec((tm, IDX_DIM), lambda i: (i, 0)),
                   pl.BlockSpec((tm, LANE), lambda i: (i, 0))],
        out_shape=[jax.ShapeDtypeStruct((t, KV_LATENT), BF16),
                   jax.ShapeDtypeStruct((t, IDX_DIM), BF16),
                   jax.ShapeDtypeStruct((t, LANE), F32)],
        compiler_params=_params("arbitrary"),
        name="small_proj",
    )(h, w_small, kv_g.reshape(1, KV_LATENT))


def _indexer_kernel(q_ref, k_ref, w_ref, bias_ref, key_scr, keyt_scr, ngt_scr, run_scr,
                    *, n_kt, k_sel):
    i = pl.program_id(1)
    q = q_ref[...]
    qs = jnp.concatenate([q[:, h * IDX_DIM:(h + 1) * IDX_DIM] for h in range(N_HEADS)], axis=0)
    w = w_ref[...]
    wcols = [w[:, h:h + 1] for h in range(N_HEADS)]
    row = lax.broadcasted_iota(I32, (Q_BLOCK, KEY_TILE), 0)
    col = lax.broadcasted_iota(I32, (Q_BLOCK, KEY_TILE), 1)
    q_chunk = (i * Q_BLOCK + row) // CHUNK
    n_used = (i * Q_BLOCK) // KEY_TILE + 1

    for j in range(n_kt):
        @pl.when(j < n_used)
        def _():
            k = k_ref[0, j * KEY_TILE:(j + 1) * KEY_TILE, :]
            logits = lax.dot_general(qs, k, (((1,), (1,)), ((), ())),
                                     preferred_element_type=F32)
            sc = wcols[0] * jnp.maximum(logits[0:Q_BLOCK], 0.0)
            for h in range(1, N_HEADS):
                sc = sc + wcols[h] * jnp.maximum(logits[h * Q_BLOCK:(h + 1) * Q_BLOCK], 0.0)
            bits = lax.bitcast_convert_type(sc, I32)
            key = bits ^ ((bits >> 31) & 0x7FFFFFFF)
            allowed = (j * KEY_TILE + col) // CHUNK <= q_chunk
            key = jnp.where(allowed, key, INT_MIN)
            key_scr[j] = key
            keyt_scr[j] = key.T

    def bisect(it, t_u):
        cand_u = t_u | jnp.left_shift(jnp.int32(1), 31 - it)
        cand_s = cand_u ^ INT_MIN

        def count(j, cnt):
            return cnt + jnp.where(keyt_scr[j] >= cand_s, 1.0, 0.0)

        cnt = lax.fori_loop(0, n_used, count, jnp.zeros((KEY_TILE, Q_BLOCK), F32))
        tot = jnp.sum(cnt, axis=0, keepdims=True)
        return jnp.where(tot >= float(k_sel), cand_u, t_u)

    t_u = lax.fori_loop(0, 32, bisect, jnp.zeros((1, Q_BLOCK), I32))
    t_s = jnp.maximum(t_u ^ INT_MIN, INT_MIN + 1)
    t_col = jnp.broadcast_to(t_s, (Q_BLOCK, Q_BLOCK)).T
    t_tile = jnp.concatenate([t_col] * (KEY_TILE // Q_BLOCK), axis=1)

    def count_ge(j, cnt):
        return cnt + jnp.where(keyt_scr[j] >= t_s, 1.0, 0.0)

    n_ge = jnp.sum(lax.fori_loop(0, n_used, count_ge, jnp.zeros((KEY_TILE, Q_BLOCK), F32)),
                   axis=0, keepdims=True)
    has_ties = jnp.max(n_ge) > float(k_sel)

    def widen(x):
        return jnp.concatenate([x] * (KEY_TILE // LANE), axis=1)

    @pl.when(jnp.logical_not(has_ties))
    def _():
        for j in range(n_kt):
            @pl.when(j < n_used)
            def _():
                bias_ref[0, 0, j] = jnp.where(key_scr[j] >= t_tile, 0.0,
                                              NEG_BIAS).astype(bias_ref.dtype)

    @pl.when(has_ties)
    def _():
        ngt_scr[...] = jnp.zeros(ngt_scr.shape, F32)
        run_scr[...] = jnp.zeros(run_scr.shape, F32)
        for j in range(n_kt):
            @pl.when(j < n_used)
            def _():
                ngt_scr[...] += jnp.sum(jnp.where(key_scr[j] > t_tile, 1.0, 0.0),
                                        axis=-1, keepdims=True)
        a = lax.broadcasted_iota(I32, (KEY_TILE, KEY_TILE), 0)
        b = lax.broadcasted_iota(I32, (KEY_TILE, KEY_TILE), 1)
        before = jnp.where(a < b, 1.0, 0.0).astype(BF16)
        room = widen(float(k_sel) - ngt_scr[...])
        for j in range(n_kt):
            @pl.when(j < n_used)
            def _():
                key = key_scr[j]
                tie = jnp.where(key == t_tile, 1.0, 0.0)
                ahead = jnp.dot(tie.astype(BF16), before, preferred_element_type=F32) \
                    + widen(run_scr[...])
                keep_tie = jnp.where(ahead < room, tie, 0.0)
                keep = jnp.where(key > t_tile, 1.0, keep_tie)
                bias_ref[0, 0, j] = jnp.where(keep > 0.5, 0.0, NEG_BIAS).astype(bias_ref.dtype)
                run_scr[...] += jnp.sum(tie, axis=-1, keepdims=True)

    for j in range(n_kt):
        @pl.when(j >= n_used)
        def _():
            bias_ref[0, 0, j] = jnp.full((Q_BLOCK, KEY_TILE), NEG_BIAS, bias_ref.dtype)


def _indexer(q_idx, k_idx3, w_idx, bsz, seq):
    n_q = seq // Q_BLOCK
    n_kt = seq // KEY_TILE
    k_sel = min(TOPK_KEYS_MAX, seq // 4)
    return pl.pallas_call(
        functools.partial(_indexer_kernel, n_kt=n_kt, k_sel=k_sel),
        grid=(bsz, n_q),
        in_specs=[pl.BlockSpec((Q_BLOCK, N_HEADS * IDX_DIM), lambda b, i: (b * n_q + i, 0)),
                  pl.BlockSpec((1, seq, IDX_DIM), lambda b, i: (b, 0, 0)),
                  pl.BlockSpec((Q_BLOCK, LANE), lambda b, i: (b * n_q + i, 0))],
        out_specs=pl.BlockSpec((1, 1, n_kt, Q_BLOCK, KEY_TILE), lambda b, i: (b, i, 0, 0, 0)),
        out_shape=jax.ShapeDtypeStruct((bsz, n_q, n_kt, Q_BLOCK, KEY_TILE), BF16),
        scratch_shapes=[pltpu.VMEM((n_kt, Q_BLOCK, KEY_TILE), I32),
                        pltpu.VMEM((n_kt, KEY_TILE, Q_BLOCK), I32),
                        pltpu.VMEM((Q_BLOCK, LANE), F32),
                        pltpu.VMEM((Q_BLOCK, LANE), F32)],
        compiler_params=_params("arbitrary", "arbitrary"),
        name="indexer",
    )(q_idx, k_idx3, w_idx)


def _attn_kernel(qa_ref, c_ref, bias_ref, wuk_ref, wuv_ref, o_ref,
                 s_scr, q_scr, m_scr, l_scr, acc_scr):
    i = pl.program_id(1)
    n_used = (i * Q_BLOCK) // KEY_TILE + 1
    rows = N_HEADS * Q_BLOCK

    qa = qa_ref[...]
    for h in range(N_HEADS):
        ql = jnp.dot(qa[:, h * HEAD_DIM:(h + 1) * HEAD_DIM], wuk_ref[h],
                     preferred_element_type=F32)
        q_scr[h * Q_BLOCK:(h + 1) * Q_BLOCK, :] = (ql * (ATTN_SCALE * LOG2_E)).astype(BF16)

    m_scr[...] = jnp.full((rows, LANE), -jnp.inf, F32)

    def scores(j, carry):
        off = pl.multiple_of(j * KEY_TILE, KEY_TILE)
        k = c_ref[0, pl.ds(off, KEY_TILE), :]
        s = lax.dot_general(q_scr[...], k, (((1,), (1,)), ((), ())),
                            preferred_element_type=F32)
        b = bias_ref[0, 0, j].astype(F32)
        for h in range(N_HEADS):
            sl = slice(h * Q_BLOCK, (h + 1) * Q_BLOCK)
            sb = s[sl] + b
            s_scr[j, sl, :] = sb
            m_scr[sl, :] = jnp.maximum(m_scr[sl, :], jnp.maximum(sb[:, :LANE], sb[:, LANE:]))
        return carry

    lax.fori_loop(0, n_used, scores, 0)

    m = jnp.max(m_scr[...], axis=-1, keepdims=True)
    l_scr[...] = jnp.zeros((rows, LANE), F32)
    acc_scr[...] = jnp.zeros((rows, KV_LATENT), F32)

    def values(j, carry):
        off = pl.multiple_of(j * KEY_TILE, KEY_TILE)
        p = jnp.exp2(s_scr[j] - m)
        l_scr[...] += p[:, :LANE] + p[:, LANE:]
        acc_scr[...] += jnp.dot(p.astype(BF16), c_ref[0, pl.ds(off, KEY_TILE), :],
                                preferred_element_type=F32)
        return carry

    lax.fori_loop(0, n_used, values, 0)

    inv_l = 1.0 / jnp.sum(l_scr[...], axis=-1, keepdims=True)
    o = (acc_scr[...] * inv_l).astype(BF16)
    for h in range(N_HEADS):
        o_ref[:, h * HEAD_DIM:(h + 1) * HEAD_DIM] = jnp.dot(
            o[h * Q_BLOCK:(h + 1) * Q_BLOCK], wuv_ref[h],
            preferred_element_type=F32).astype(o_ref.dtype)


def _attention(q_a, c_kv3, bias, w_uk, w_uv, bsz, seq):
    n_q = seq // Q_BLOCK
    n_kt = seq // KEY_TILE
    rows = N_HEADS * Q_BLOCK
    t = bsz * seq
    return pl.pallas_call(
        _attn_kernel,
        grid=(bsz, n_q),
        in_specs=[pl.BlockSpec((Q_BLOCK, N_HEADS * HEAD_DIM), lambda b, i: (b * n_q + i, 0)),
                  pl.BlockSpec((1, seq, KV_LATENT), lambda b, i: (b, 0, 0)),
                  pl.BlockSpec((1, 1, n_kt, Q_BLOCK, KEY_TILE), lambda b, i: (b, i, 0, 0, 0)),
                  pl.BlockSpec((N_HEADS, HEAD_DIM, KV_LATENT), lambda b, i: (0, 0, 0)),
                  pl.BlockSpec((N_HEADS, KV_LATENT, HEAD_DIM), lambda b, i: (0, 0, 0))],
        out_specs=pl.BlockSpec((Q_BLOCK, N_HEADS * HEAD_DIM), lambda b, i: (b * n_q + i, 0)),
        out_shape=jax.ShapeDtypeStruct((t, N_HEADS * HEAD_DIM), BF16),
        scratch_shapes=[pltpu.VMEM((n_kt, rows, KEY_TILE), F32),
                        pltpu.VMEM((rows, KV_LATENT), BF16),
                        pltpu.VMEM((rows, LANE), F32),
                        pltpu.VMEM((rows, LANE), F32),
                        pltpu.VMEM((rows, KV_LATENT), F32)],
        compiler_params=_params("arbitrary", "arbitrary"),
        name="attention",
    )(q_a, c_kv3, bias, w_uk, w_uv)


def _spatial_kernel(z_ref, g_ref, ws_ref, bs_ref, o_ref, *, width):
    z = z_ref[...].astype(F32)
    z = 0.5 * z * (1.0 + lax.erf(z * (2.0 ** -0.5)))
    u = z[:, :width]
    v = z[:, width:]
    mu = jnp.mean(v, axis=-1, keepdims=True)
    vc = v - mu
    var = jnp.mean(vc * vc, axis=-1, keepdims=True)
    vn = (vc * lax.rsqrt(var + EPS) * g_ref[...]).astype(BF16)
    r = lax.broadcasted_iota(I32, (SG_CHUNK, SG_CHUNK), 0)
    c = lax.broadcasted_iota(I32, (SG_CHUNK, SG_CHUNK), 1)
    gd = width // N_GROUPS_B
    bs = bs_ref[...]
    for g in range(N_GROUPS_B):
        wg = jnp.where(r >= c, ws_ref[g], 0.0).astype(BF16)
        s = jnp.dot(wg, vn[:, g * gd:(g + 1) * gd], preferred_element_type=F32) + bs[:, g:g + 1]
        o_ref[:, g * gd:(g + 1) * gd] = (u[:, g * gd:(g + 1) * gd] * s).astype(o_ref.dtype)


def _spatial_gating(z, sg_g, w_spatial, b_spatial_t):
    t, w2 = z.shape
    width = w2 // 2
    return pl.pallas_call(
        functools.partial(_spatial_kernel, width=width),
        grid=(t // SG_CHUNK,),
        in_specs=[pl.BlockSpec((SG_CHUNK, w2), lambda i: (i, 0)),
                  pl.BlockSpec((1, width), lambda i: (0, 0)),
                  pl.BlockSpec((N_GROUPS_B, SG_CHUNK, SG_CHUNK), lambda i: (0, 0, 0)),
                  pl.BlockSpec((SG_CHUNK, LANE), lambda i: (0, 0))],
        out_specs=pl.BlockSpec((SG_CHUNK, width), lambda i: (i, 0)),
        out_shape=jax.ShapeDtypeStruct((t, width), BF16),
        compiler_params=_params("arbitrary"),
        name="spatial_gating",
    )(z, sg_g.reshape(1, width), w_spatial, b_spatial_t)


def _mix_kernel(a_ref, b_ref, wa_ref, wb_ref, ga_ref, gb_ref, o_ref):
    ya = jnp.dot(a_ref[...], wa_ref[...], preferred_element_type=F32)
    yb = jnp.dot(b_ref[...], wb_ref[...], preferred_element_type=F32)
    o_ref[...] = (ga_ref[...].astype(F32) * ya + gb_ref[...].astype(F32) * yb).astype(o_ref.dtype)


def _mix(o_a, sg, w_a, w_b, gates):
    t, k = o_a.shape
    d = w_a.shape[1]
    tm, tn = 1024, 1024
    nb = d // tn
    return pl.pallas_call(
        _mix_kernel,
        grid=(nb, t // tm),
        in_specs=[pl.BlockSpec((tm, k), lambda j, i: (i, 0)),
                  pl.BlockSpec((tm, k), lambda j, i: (i, 0)),
                  pl.BlockSpec((k, tn), lambda j, i: (0, j)),
                  pl.BlockSpec((k, tn), lambda j, i: (0, j)),
                  pl.BlockSpec((tm, tn), lambda j, i: (i, j)),
                  pl.BlockSpec((tm, tn), lambda j, i: (i, nb + j))],
        out_specs=pl.BlockSpec((tm, tn), lambda j, i: (i, j)),
        out_shape=jax.ShapeDtypeStruct((t, d), BF16),
        compiler_params=_params("arbitrary", "arbitrary"),
        name="branch_mix",
    )(o_a, sg, w_a, w_b, gates, gates)


def _out_router_kernel(a_ref, w_ref, x_ref, mod_ref, g_ref, wr_ref, br_ref,
                       x1_ref, h2_ref, lg_ref):
    mix = jnp.dot(a_ref[...], w_ref[...], preferred_element_type=F32)
    x1 = x_ref[...] + mod_ref[0, 2:3, :] * mix
    x1_ref[...] = x1
    y = x1 * lax.rsqrt(jnp.mean(x1 * x1, axis=-1, keepdims=True) + EPS) * g_ref[...]
    h2 = y * (1.0 + mod_ref[0, 4:5, :]) + mod_ref[0, 3:4, :]
    h2_ref[...] = h2
    h_hi = h2.astype(BF16)
    h_lo = (h2 - h_hi.astype(F32)).astype(BF16)
    wr = wr_ref[...]
    w_hi = wr.astype(BF16)
    w_lo = (wr - w_hi.astype(F32)).astype(BF16)
    lg = jnp.dot(h_hi, w_hi, preferred_element_type=F32)
    lg = lg + jnp.dot(h_lo, w_hi, preferred_element_type=F32)
    lg = lg + jnp.dot(h_hi, w_lo, preferred_element_type=F32)
    lg_ref[...] = lg + br_ref[...]


def _out_router(mixpre, w_out, x2, mod6, norm2_g, w_router_p, b_router_p, seq):
    t, d = x2.shape
    tm = 256
    return pl.pallas_call(
        _out_router_kernel,
        grid=(t // tm,),
        in_specs=[pl.BlockSpec((tm, d), lambda i: (i, 0)),
                  pl.BlockSpec((d, d), lambda i: (0, 0)),
                  pl.BlockSpec((tm, d), lambda i: (i, 0)),
                  pl.BlockSpec((1, 6, d), lambda i: ((i * tm) // seq, 0, 0)),
                  pl.BlockSpec((1, d), lambda i: (0, 0)),
                  pl.BlockSpec((d, LANE), lambda i: (0, 0)),
                  pl.BlockSpec((1, LANE), lambda i: (0, 0))],
        out_specs=[pl.BlockSpec((tm, d), lambda i: (i, 0)),
                   pl.BlockSpec((tm, d), lambda i: (i, 0)),
                   pl.BlockSpec((tm, LANE), lambda i: (i, 0))],
        out_shape=[jax.ShapeDtypeStruct((t, d), F32),
                   jax.ShapeDtypeStruct((t, d), F32),
                   jax.ShapeDtypeStruct((t, LANE), F32)],
        compiler_params=_params("arbitrary"),
        name="out_proj_router",
    )(mixpre, w_out, x2, mod6, norm2_g.reshape(1, d), w_router_p, b_router_p)


DISPATCH_TOKENS = 512


def _pack_bf16_pairs(x):
    half = x.shape[1] // 2
    hi = lax.bitcast_convert_type(x[:, :half].astype(BF16).astype(F32), I32)
    lo = lax.bitcast_convert_type(x[:, half:].astype(BF16).astype(F32), I32)
    return hi | lax.shift_right_logical(lo, 16)


def _unpack_bf16_pairs(p):
    hi = lax.bitcast_convert_type(p & jnp.int32(-65536), F32).astype(BF16)
    lo = lax.bitcast_convert_type(lax.shift_left(p, 16), F32).astype(BF16)
    return hi, lo


def _dispatch_kernel(pos_ref, pad_ref, h_ref, xs_hbm, pk_scr, zero_scr, sem, *, n_alloc_blocks):
    i = pl.program_id(0)
    n_tiles = pl.num_programs(0)
    packed = _pack_bf16_pairs(h_ref[...])
    for c in range(pk_scr.shape[1]):
        pk_scr[:, c, :] = packed[:, c * LANE:(c + 1) * LANE]
    zero_scr[...] = jnp.zeros(zero_scr.shape, zero_scr.dtype)

    def row_copy(t, k):
        return pltpu.make_async_copy(
            pk_scr.at[t], xs_hbm.at[pos_ref[0, 0, t * TOP_K + k]], sem)

    def pad_copy(r):
        return pltpu.make_async_copy(zero_scr.at[0], xs_hbm.at[r], sem)

    def tail_copy(b):
        off = pl.multiple_of(b * MOE_ROWS, MOE_ROWS)
        return pltpu.make_async_copy(zero_scr, xs_hbm.at[pl.ds(off, MOE_ROWS)], sem)

    e_lo = (i * N_EXPERTS) // n_tiles
    e_hi = ((i + 1) * N_EXPERTS) // n_tiles
    tail_lo = jnp.where(i == 0, pad_ref[2, 0], n_alloc_blocks)

    def for_rows(fn):
        def per_token(t, carry):
            for k in range(TOP_K):
                fn(row_copy(t, k), k)
            return carry

        def per_expert(e, carry):
            def per_pad(r, c):
                fn(pad_copy(r), 0)
                return c
            return lax.fori_loop(pad_ref[0, e], pad_ref[1, e], per_pad, carry)

        def per_tail(b, carry):
            fn(tail_copy(b), 0)
            return carry

        lax.fori_loop(0, DISPATCH_TOKENS, per_token, 0)
        lax.fori_loop(e_lo, e_hi, per_expert, 0)
        lax.fori_loop(tail_lo, n_alloc_blocks, per_tail, 0)

    for_rows(lambda cp, k: cp.start(priority=k % 2))
    for_rows(lambda cp, k: cp.wait())


def _dispatch(pos, pad_rng, h2, n_alloc_rows):
    t, d = h2.shape
    n_tiles = t // DISPATCH_TOKENS
    pos3 = pos.reshape(n_tiles, 1, DISPATCH_TOKENS * TOP_K)
    return pl.pallas_call(
        functools.partial(_dispatch_kernel, n_alloc_blocks=n_alloc_rows // MOE_ROWS),
        grid=(n_tiles,),
        in_specs=[pl.BlockSpec((1, 1, DISPATCH_TOKENS * TOP_K), lambda i: (i, 0, 0),
                               memory_space=pltpu.SMEM),
                  pl.BlockSpec(memory_space=pltpu.SMEM),
                  pl.BlockSpec((DISPATCH_TOKENS, d), lambda i: (i, 0))],
        out_specs=pl.BlockSpec(memory_space=pl.ANY),
        out_shape=jax.ShapeDtypeStruct((n_alloc_rows, d // (2 * LANE), LANE), I32),
        scratch_shapes=[pltpu.VMEM((DISPATCH_TOKENS, d // (2 * LANE), LANE), I32),
                        pltpu.VMEM((MOE_ROWS, d // (2 * LANE), LANE), I32),
                        pltpu.SemaphoreType.DMA(())],
        compiler_params=_params("arbitrary"),
        name="moe_dispatch",
    )(pos3, pad_rng, h2)


def _ffn_kernel(ie_ref, in_ref, ix_ref, io_ref, no_ref, xs_hbm, wgu_hbm, wdn_hbm, bg_ref, bl_ref,
                bd_ref, o_hbm, x_scr, xw_scr, acc_scr, wg_buf, wl_buf, wd_buf, sem, sem_x, sem_w):
    i = pl.program_id(0)
    f = pl.program_id(1)
    n_items = pl.num_programs(0)
    n_f = pl.num_programs(1)
    n_sub = in_ref[i]
    half = x_scr.shape[1] // 2
    r = MOE_ROWS
    win = xw_scr.shape[1]
    tf = wg_buf.shape[2]
    d_ff = wdn_hbm.shape[1]
    o_ref = acc_scr.at[i % 2]
    step = i * n_f + f
    ahead = WEIGHT_SLOTS - 1

    def for_weights(g, fn):
        item = g // n_f
        ft = g - item * n_f
        e = ie_ref[item]
        slot = g % WEIGHT_SLOTS
        col = pl.multiple_of(ft * tf, tf)
        fn(pltpu.make_async_copy(wgu_hbm.at[e, :, pl.ds(col, tf)], wg_buf.at[slot],
                                 sem_w.at[slot]))
        fn(pltpu.make_async_copy(wgu_hbm.at[e, :, pl.ds(d_ff + col, tf)], wl_buf.at[slot],
                                 sem_w.at[slot]))
        fn(pltpu.make_async_copy(wdn_hbm.at[e, pl.ds(col, tf), :], wd_buf.at[slot],
                                 sem_w.at[slot]))

    def fetch_if_used(g):
        @pl.when(g < n_items * n_f)
        def _():
            @pl.when(in_ref[g // n_f] > 0)
            def _():
                for_weights(g, lambda cp: cp.start())

    @pl.when(step == 0)
    def _():
        for g in range(ahead):
            fetch_if_used(g)

    fetch_if_used(step + ahead)

    @pl.when(n_sub > 0)
    def _():
        for_weights(step, lambda cp: cp.wait())

    w_slot = step % WEIGHT_SLOTS

    def for_window(item, fn):
        off = pl.multiple_of(ix_ref[item] * r, r)
        for c in range(half // LANE):
            fn(pltpu.make_async_copy(xs_hbm.at[pl.ds(off, win), c, :],
                                     xw_scr.at[item % 2, :, pl.ds(c * LANE, LANE)],
                                     sem_x.at[item % 2]))

    def out_copy(item, s):
        off = pl.multiple_of(io_ref[item] * r, r)
        return pltpu.make_async_copy(acc_scr.at[item % 2, pl.ds(0, s * r), pl.ds(0, half)],
                                     o_hbm.at[pl.ds(off, s * r), :], sem.at[item % 2])

    def for_out(item, fn):
        for s in range(1, ITEM_SUB + 1):
            @pl.when(no_ref[item] == s)
            def _():
                fn(out_copy(item, s))

    @pl.when(f == 0)
    def _():
        @pl.when(i == 0)
        def _():
            for_window(0, lambda cp: cp.start())

        @pl.when(i + 1 < n_items)
        def _():
            for_window(i + 1, lambda cp: cp.start())

        for_window(i, lambda cp: cp.wait())

        @pl.when(i >= 2)
        def _():
            for_out(i - 2, lambda cp: cp.wait())
        o_ref[...] = jnp.broadcast_to(bd_ref[0], o_ref.shape)

    def unpack(rows):
        hi, lo = _unpack_bf16_pairs(xw_scr[i % 2, :rows, :])
        x_scr[:rows, :half] = hi
        x_scr[:rows, half:] = lo

    def ffn(xb):
        g = jnp.dot(xb, wg_buf[w_slot].astype(BF16), preferred_element_type=F32) + bg_ref[0]
        lin = jnp.dot(xb, wl_buf[w_slot].astype(BF16), preferred_element_type=F32) + bl_ref[0]
        g = jnp.minimum(g, SWIGLU_LIMIT)
        lin = jnp.clip(lin, -SWIGLU_LIMIT, SWIGLU_LIMIT)
        act = g * jax.nn.sigmoid(SWIGLU_ALPHA * g) * (lin + 1.0)
        return jnp.dot(act.astype(BF16), wd_buf[w_slot].astype(BF16), preferred_element_type=F32)

    for s in range(1, ITEM_SUB + 1):
        rows = s * r

        @pl.when(jnp.logical_and(n_sub == s, f == 0))
        def _():
            unpack(rows)

        @pl.when(n_sub == s)
        def _():
            o_ref[:rows, :] += ffn(x_scr[:rows, :])

    @pl.when(f == n_f - 1)
    def _():
        o_ref[:, :half] = lax.bitcast_convert_type(_pack_bf16_pairs(o_ref[...]), F32)
        for_out(i, lambda cp: cp.start())

        @pl.when(i == n_items - 1)
        def _():
            @pl.when(i >= 1)
            def _():
                for_out(i - 1, lambda cp: cp.wait())
            for_out(i, lambda cp: cp.wait())


def _expert_ffn(items, xs, w_gate_up, b_gate_up3, w_down, b_down3):
    p = xs.shape[0]
    dh = xs.shape[1] * xs.shape[2]
    d = 2 * dh
    n_e, _, f2 = w_gate_up.shape
    d_ff = f2 // 2
    tf = FFN_TILE
    nf = d_ff // tf
    n_items = items[0].shape[0]
    win = ITEM_SUB * MOE_ROWS

    def ft(f, ni, i):
        return jnp.where(ni[i] > 0, f, nf - 1)

    grid_spec = pltpu.PrefetchScalarGridSpec(
        num_scalar_prefetch=5,
        grid=(n_items, nf),
        in_specs=[pl.BlockSpec(memory_space=pl.ANY),
                  pl.BlockSpec(memory_space=pl.ANY),
                  pl.BlockSpec(memory_space=pl.ANY),
                  pl.BlockSpec((1, 1, tf), lambda i, f, ie, ni, ix, io, no: (ie[i], 0, ft(f, ni, i))),
                  pl.BlockSpec((1, 1, tf),
                               lambda i, f, ie, ni, ix, io, no: (ie[i], 0, nf + ft(f, ni, i))),
                  pl.BlockSpec((1, 1, d), lambda i, f, ie, ni, ix, io, no: (ie[i], 0, 0))],
        out_specs=pl.BlockSpec(memory_space=pl.ANY),
        scratch_shapes=[pltpu.VMEM((win, d), BF16),
                        pltpu.VMEM((2, win, dh), I32),
                        pltpu.VMEM((2, win, d), F32),
                        pltpu.VMEM((WEIGHT_SLOTS, d, tf), F32),
                        pltpu.VMEM((WEIGHT_SLOTS, d, tf), F32),
                        pltpu.VMEM((WEIGHT_SLOTS, tf, d), F32),
                        pltpu.SemaphoreType.DMA((2,)),
                        pltpu.SemaphoreType.DMA((2,)),
                        pltpu.SemaphoreType.DMA((WEIGHT_SLOTS,))],
    )
    return pl.pallas_call(
        _ffn_kernel,
        grid_spec=grid_spec,
        out_shape=jax.ShapeDtypeStruct((p, dh), F32),
        compiler_params=_params("arbitrary", "arbitrary"),
        name="expert_ffn",
    )(*items, xs, w_gate_up, w_down, b_gate_up3, b_gate_up3, b_down3)


COMBINE_TOKENS = 256


def _combine_kernel(pos_ref, pos_next_ref, ys_hbm, gate_ref, x1_ref, mod_ref, g_ref, o_ref,
                    buf, sem):
    i = pl.program_id(0)
    n = pl.num_programs(0)
    slot = i % 2

    def issue(p_ref, sl):
        def body(t, carry):
            for k in range(TOP_K):
                pltpu.make_async_copy(ys_hbm.at[pl.ds(p_ref[0, 0, t * TOP_K + k], 1), :],
                                      buf.at[sl, k, pl.ds(t, 1), :],
                                      sem.at[sl]).start(priority=k % 2)
            return carry
        lax.fori_loop(0, COMBINE_TOKENS, body, 0)

    def wait_all(sl):
        def body(t, carry):
            for k in range(TOP_K):
                pltpu.make_async_copy(ys_hbm.at[pl.ds(0, 1), :],
                                      buf.at[sl, k, pl.ds(t, 1), :], sem.at[sl]).wait()
            return carry
        lax.fori_loop(0, COMBINE_TOKENS, body, 0)

    @pl.when(i == 0)
    def _():
        issue(pos_ref, 0)

    @pl.when(i + 1 < n)
    def _():
        issue(pos_next_ref, 1 - slot)

    wait_all(slot)

    gate = gate_ref[...]
    d = x1_ref.shape[1]
    half = d // 2
    moe_hi = jnp.zeros((COMBINE_TOKENS, half), F32)
    moe_lo = jnp.zeros((COMBINE_TOKENS, half), F32)
    for k in range(TOP_K):
        bits = lax.bitcast_convert_type(buf[slot, k], I32)
        hi = lax.bitcast_convert_type(bits & jnp.int32(-65536), F32)
        lo = lax.bitcast_convert_type(lax.shift_left(bits, 16), F32)
        moe_hi = moe_hi + gate[:, k:k + 1] * hi
        moe_lo = moe_lo + gate[:, k:k + 1] * lo
    x_hi = x1_ref[:, :half] + mod_ref[0, 5:6, :half] * moe_hi
    x_lo = x1_ref[:, half:] + mod_ref[0, 5:6, half:] * moe_lo
    ss = jnp.sum(x_hi * x_hi, axis=-1, keepdims=True) + jnp.sum(x_lo * x_lo, axis=-1, keepdims=True)
    inv = lax.rsqrt(ss * (1.0 / d) + EPS)
    o_ref[:, :half] = x_hi * inv * g_ref[:, :half]
    o_ref[:, half:] = x_lo * inv * g_ref[:, half:]


def _combine(pos, ys, gates_p, x1, mod6, final_g, seq):
    t, d = x1.shape
    tt = COMBINE_TOKENS
    n_tiles = t // tt
    pos3 = pos.reshape(n_tiles, 1, tt * TOP_K)
    return pl.pallas_call(
        _combine_kernel,
        grid=(n_tiles,),
        in_specs=[pl.BlockSpec((1, 1, tt * TOP_K), lambda i: (i, 0, 0), memory_space=pltpu.SMEM),
                  pl.BlockSpec((1, 1, tt * TOP_K), lambda i: (jnp.minimum(i + 1, n_tiles - 1), 0, 0),
                               memory_space=pltpu.SMEM),
                  pl.BlockSpec(memory_space=pl.ANY),
                  pl.BlockSpec((tt, LANE), lambda i: (i, 0)),
                  pl.BlockSpec((tt, d), lambda i: (i, 0)),
                  pl.BlockSpec((1, 6, d), lambda i: ((i * tt) // seq, 0, 0)),
                  pl.BlockSpec((1, d), lambda i: (0, 0))],
        out_specs=pl.BlockSpec((tt, d), lambda i: (i, 0)),
        out_shape=jax.ShapeDtypeStruct((t, d), F32),
        scratch_shapes=[pltpu.VMEM((2, TOP_K, tt, d // 2), F32),
                        pltpu.SemaphoreType.DMA((2,))],
        compiler_params=_params("arbitrary"),
        name="moe_combine",
    )(pos3, pos3, ys, gates_p, x1, mod6, final_g.reshape(1, d))


ROUTE_TOKENS = 512


def _route_kernel(lg_ref, idx_ref, gate_ref, rank_ref, cnt_ref, carry_scr):
    i = pl.program_id(0)
    tt = lg_ref.shape[0]

    @pl.when(i == 0)
    def _():
        carry_scr[...] = jnp.zeros(carry_scr.shape, F32)

    lane = lax.broadcasted_iota(I32, (tt, LANE), 1)
    v = jnp.where(lane < N_EXPERTS, lg_ref[...], -jnp.inf)
    tops, ids, hits = [], [], []
    for _ in range(TOP_K):
        m = jnp.max(v, axis=-1, keepdims=True)
        idx = jnp.min(jnp.where(v == m, lane, LANE), axis=-1, keepdims=True)
        hit = lane == idx
        v = jnp.where(hit, -jnp.inf, v)
        tops.append(m)
        ids.append(idx)
        hits.append(hit)

    exps = [jnp.exp(m - tops[0]) for m in tops]
    total = exps[0]
    for e in exps[1:]:
        total = total + e

    sel = jnp.where(hits[0], 1.0, 0.0)
    for hit in hits[1:]:
        sel = sel + jnp.where(hit, 1.0, 0.0)
    r = lax.broadcasted_iota(I32, (tt, tt), 0)
    c = lax.broadcasted_iota(I32, (tt, tt), 1)
    earlier = jnp.where(c < r, 1.0, 0.0).astype(BF16)
    rank_all = jnp.dot(earlier, sel.astype(BF16), preferred_element_type=F32) + carry_scr[...]
    carry_scr[...] += jnp.sum(sel, axis=0, keepdims=True)
    cnt_ref[...] = carry_scr[...]

    idx_out = jnp.zeros((tt, LANE), I32)
    gate_out = jnp.zeros((tt, LANE), F32)
    rank_out = jnp.zeros((tt, LANE), F32)
    for k in range(TOP_K):
        rk = jnp.sum(jnp.where(hits[k], rank_all, 0.0), axis=-1, keepdims=True)
        idx_out = jnp.where(lane == k, ids[k], idx_out)
        gate_out = jnp.where(lane == k, exps[k] / total, gate_out)
        rank_out = jnp.where(lane == k, rk, rank_out)
    idx_ref[...] = idx_out
    gate_ref[...] = gate_out
    rank_ref[...] = rank_out.astype(I32)


def _route(logits_p):
    t = logits_p.shape[0]
    tt = ROUTE_TOKENS
    return pl.pallas_call(
        _route_kernel,
        grid=(t // tt,),
        in_specs=[pl.BlockSpec((tt, LANE), lambda i: (i, 0))],
        out_specs=[pl.BlockSpec((tt, LANE), lambda i: (i, 0)),
                   pl.BlockSpec((tt, LANE), lambda i: (i, 0)),
                   pl.BlockSpec((tt, LANE), lambda i: (i, 0)),
                   pl.BlockSpec((1, LANE), lambda i: (0, 0))],
        out_shape=[jax.ShapeDtypeStruct((t, LANE), I32),
                   jax.ShapeDtypeStruct((t, LANE), F32),
                   jax.ShapeDtypeStruct((t, LANE), I32),
                   jax.ShapeDtypeStruct((1, LANE), F32)],
        scratch_shapes=[pltpu.VMEM((1, LANE), F32)],
        compiler_params=_params("arbitrary"),
        name="moe_route",
    )(logits_p)


def _routing(logits_p):
    t = logits_p.shape[0]
    idx_p, gates_p, rank_p, cnt = _route(logits_p)
    top_idx = idx_p[:, :TOP_K]
    counts = cnt[0, :N_EXPERTS].astype(I32)
    padded = ((counts + MOE_ROWS - 1) // MOE_ROWS) * MOE_ROWS
    pad_ends = jnp.cumsum(padded)
    pstarts = pad_ends - padded
    pos = (pstarts[top_idx] + rank_p[:, :TOP_K]).astype(I32)

    win = ITEM_SUB * MOE_ROWS
    n_rows = t * TOP_K + N_EXPERTS * MOE_ROWS
    n_alloc = n_rows + win
    n_items = (n_rows // MOE_ROWS + N_EXPERTS * (ITEM_SUB - 1)) // ITEM_SUB + 1

    tail0 = pad_ends[-1]
    pad_rng = jnp.stack([pstarts + counts, pad_ends,
                         jnp.full((N_EXPERTS,), tail0 // MOE_ROWS)]).astype(I32)

    nb_e = padded // MOE_ROWS
    items_e = (nb_e + ITEM_SUB - 1) // ITEM_SUB
    item_ends = jnp.cumsum(items_e)
    item_starts = item_ends - items_e
    n_used = item_ends[-1]
    q = jnp.arange(n_items, dtype=I32)
    e_q = jnp.clip(jnp.searchsorted(item_ends, q, side="right"), 0, N_EXPERTS - 1)
    local = q - item_starts[e_q]
    used = q < n_used
    item_n = jnp.where(used, jnp.minimum(ITEM_SUB, nb_e[e_q] - ITEM_SUB * local), 0)
    item_x = jnp.where(used, pstarts[e_q] // MOE_ROWS + local * ITEM_SUB, 0)
    n_alloc_blocks = n_alloc // MOE_ROWS
    tail_blk = tail0 // MOE_ROWS + (q - n_used) * ITEM_SUB
    item_o = jnp.where(used, item_x, jnp.minimum(tail_blk, n_alloc_blocks - 1))
    item_no = jnp.where(used, item_n, jnp.clip(n_alloc_blocks - tail_blk, 0, ITEM_SUB))
    e_last = jnp.max(jnp.where(items_e > 0, jnp.arange(N_EXPERTS, dtype=I32), 0))
    item_e = jnp.where(used, e_q, e_last)
    items = tuple(a.astype(I32) for a in (item_e, item_n, item_x, item_o, item_no))
    return pos, gates_p, pad_rng, items, n_alloc


def _layer(x2, c_act_in, bsz, seq, w_mod, b_mod, norm1_g, w_in, kv_norm_g, w_uk, w_uv,
           w_proj_a, sg_norm_g, w_spatial, b_spatial, w_proj_b, w_out, norm2_g,
           w_router, b_router, w_gate_up, b_gate_up, w_down, b_down):
    t, d = x2.shape
    w_qa = N_HEADS * HEAD_DIM
    w_qi = N_HEADS * IDX_DIM
    width_b = sg_norm_g.shape[0]

    mod = _modulation(c_act_in, w_mod, b_mod)
    mod6 = mod.reshape(bsz, 6, d)
    h = _norm_mod(x2, norm1_g, mod6[:, :3], seq)

    o0 = w_qa
    o1 = o0 + KV_LATENT
    o2 = o1 + w_qi
    o3 = o2 + IDX_DIM
    o4 = o3 + N_HEADS
    o5 = o4 + 2 * width_b
    w_small = jnp.concatenate(
        [w_in[:, o0:o1], w_in[:, o2:o3], w_in[:, o3:o4],
         jnp.zeros((d, LANE - N_HEADS), w_in.dtype)], axis=1).astype(BF16)

    q_a = _matmul(h, w_in[:, :o0].astype(BF16), name="proj_q_a")
    q_idx = _matmul(h, w_in[:, o1:o2].astype(BF16), name="proj_q_idx")
    z_b = _matmul(h, w_in[:, o4:o5].astype(BF16), name="proj_z_b")
    gates = _matmul(h, w_in[:, o5:].astype(BF16), act="sigmoid", name="proj_gates")
    c_kv, k_idx, w_idx = _small_proj(h, w_small, kv_norm_g)

    bias = _indexer(q_idx, k_idx.reshape(bsz, seq, IDX_DIM), w_idx, bsz, seq)
    o_a = _attention(q_a, c_kv.reshape(bsz, seq, KV_LATENT), bias,
                     w_uk.astype(BF16), w_uv.astype(BF16), bsz, seq)

    b_sp_t = jnp.pad(jnp.transpose(b_spatial), ((0, 0), (0, LANE - N_GROUPS_B)))
    sg = _spatial_gating(z_b, sg_norm_g, w_spatial, b_sp_t)

    mixpre = _mix(o_a, sg, w_proj_a.astype(BF16), w_proj_b.astype(BF16), gates)

    w_router_p = jnp.pad(w_router, ((0, 0), (0, LANE - N_EXPERTS)))
    b_router_p = jnp.pad(b_router, (0, LANE - N_EXPERTS)).reshape(1, LANE)
    x1, h2, logits_p = _out_router(mixpre, w_out.astype(BF16), x2, mod6, norm2_g,
                                   w_router_p, b_router_p, seq)

    pos, gates_p, pad_rng, items, n_alloc = _routing(logits_p)
    xs = _dispatch(pos.reshape(-1), pad_rng, h2, n_alloc)
    n_e = w_gate_up.shape[0]
    ys = _expert_ffn(items, xs, w_gate_up, b_gate_up.reshape(n_e, 1, -1),
                     w_down, b_down.reshape(n_e, 1, -1))
    return pos, ys, gates_p, x1, mod6


def kernel(x, c, w_mod, b_mod, norm1_g, w_in, kv_norm_g, w_uk, w_uv, w_proj_a, sg_norm_g,
           w_spatial, b_spatial, w_proj_b, w_out, norm2_g, w_router, b_router, w_gate_up,
           b_gate_up, w_down, b_down, final_g):
    bsz, seq, d = x.shape
    depth = w_mod.shape[0]
    assert depth == 1, "single-layer block"
    x2 = x.reshape(bsz * seq, d)
    pos, ys, gates_p, x1, mod6 = _layer(
        x2, c, bsz, seq, w_mod[0], b_mod[0], norm1_g[0], w_in[0], kv_norm_g[0], w_uk[0],
        w_uv[0], w_proj_a[0], sg_norm_g[0], w_spatial[0], b_spatial[0], w_proj_b[0],
        w_out[0], norm2_g[0], w_router[0], b_router[0], w_gate_up[0], b_gate_up[0],
        w_down[0], b_down[0])
    out = _combine(pos.reshape(-1), ys, gates_p, x1, mod6, final_g, seq)
    return out.reshape(bsz, seq, d)
```

```python
import functools

import jax
import jax.numpy as jnp
from jax import lax
from jax.experimental import pallas as pl
from jax.experimental.pallas import tpu as pltpu

F32 = jnp.float32
BF16 = jnp.bfloat16
I32 = jnp.int32

EPS = 1e-6
CHUNK = 64
N_HEADS = 16
HEAD_DIM = 128
KV_LATENT = 256
IDX_DIM = 128
TOPK_KEYS_MAX = 256
Q_BLOCK = 128
KEY_TILE = 256
N_GROUPS_B = 8
SG_CHUNK = 128
N_EXPERTS = 32
TOP_K = 4
SWIGLU_ALPHA = 1.702
SWIGLU_LIMIT = 7.0
ATTN_SCALE = HEAD_DIM ** -0.5
LOG2_E = 1.4426950408889634
IDX_W_SCALE = (N_HEADS ** -0.5) * (IDX_DIM ** -0.5)

LANE = 128
COUNT_ROWS = 64
MOE_ROWS = 512
ITEM_SUB = 2
FFN_TILE = 256
WEIGHT_SLOTS = 3
NEG_BIAS = -1e30
INT_MIN = -2147483648
VMEM_LIMIT = 56 * 1024 * 1024


def _params(*sem):
    return pltpu.CompilerParams(dimension_semantics=sem, vmem_limit_bytes=VMEM_LIMIT)


def _mod_kernel(c_ref, w_ref, b_ref, o_ref):
    c = c_ref[...]
    ca = (c * jax.nn.sigmoid(c)).astype(BF16)
    o_ref[...] = jnp.dot(ca, w_ref[...].astype(BF16), preferred_element_type=F32) + b_ref[...]


def _modulation(c, w_mod, b_mod):
    bsz, d = c.shape
    n = w_mod.shape[1]
    tn = 1024
    return pl.pallas_call(
        _mod_kernel,
        grid=(n // tn,),
        in_specs=[pl.BlockSpec((bsz, d), lambda j: (0, 0)),
                  pl.BlockSpec((d, tn), lambda j: (0, j)),
                  pl.BlockSpec((1, tn), lambda j: (0, j))],
        out_specs=pl.BlockSpec((bsz, tn), lambda j: (0, j)),
        out_shape=jax.ShapeDtypeStruct((bsz, n), F32),
        compiler_params=_params("arbitrary"),
        name="modulation",
    )(c, w_mod, b_mod.reshape(1, n))


def _norm_mod_kernel(x_ref, g_ref, mod_ref, o_ref):
    x = x_ref[...]
    y = x * lax.rsqrt(jnp.mean(x * x, axis=-1, keepdims=True) + EPS) * g_ref[...]
    o_ref[...] = (y * (1.0 + mod_ref[0, 1:2, :]) + mod_ref[0, 0:1, :]).astype(o_ref.dtype)


def _norm_mod(x2, g, mod3, seq):
    t, d = x2.shape
    tm = 512
    return pl.pallas_call(
        _norm_mod_kernel,
        grid=(t // tm,),
        in_specs=[pl.BlockSpec((tm, d), lambda i: (i, 0)),
                  pl.BlockSpec((1, d), lambda i: (0, 0)),
                  pl.BlockSpec((1, 3, d), lambda i: ((i * tm) // seq, 0, 0))],
        out_specs=pl.BlockSpec((tm, d), lambda i: (i, 0)),
        out_shape=jax.ShapeDtypeStruct((t, d), BF16),
        compiler_params=_params("arbitrary"),
        name="norm_mod",
    )(x2, g.reshape(1, d), mod3)


def _mm_kernel(a_ref, w_ref, o_ref, *, act):
    acc = jnp.dot(a_ref[...], w_ref[...], preferred_element_type=F32)
    if act == "sigmoid":
        acc = jax.nn.sigmoid(acc)
    o_ref[...] = acc.astype(o_ref.dtype)


def _matmul(a, w, *, act=None, out_dtype=BF16, tm=1024, tn=1024, name="matmul"):
    m, k = a.shape
    n = w.shape[1]
    return pl.pallas_call(
        functools.partial(_mm_kernel, act=act),
        grid=(n // tn, m // tm),
        in_specs=[pl.BlockSpec((tm, k), lambda j, i: (i, 0)),
                  pl.BlockSpec((k, tn), lambda j, i: (0, j))],
        out_specs=pl.BlockSpec((tm, tn), lambda j, i: (i, j)),
        out_shape=jax.ShapeDtypeStruct((m, n), out_dtype),
        compiler_params=_params("arbitrary", "arbitrary"),
        name=name,
    )(a, w)


def _small_proj_kernel(a_ref, w_ref, g_ref, ckv_ref, kidx_ref, widx_ref):
    acc = jnp.dot(a_ref[...], w_ref[...], preferred_element_type=F32)
    c = acc[:, :KV_LATENT]
    cn = c * lax.rsqrt(jnp.mean(c * c, axis=-1, keepdims=True) + EPS) * g_ref[...]
    ckv_ref[...] = cn.astype(ckv_ref.dtype)
    kidx_ref[...] = acc[:, KV_LATENT:KV_LATENT + IDX_DIM].astype(kidx_ref.dtype)
    widx_ref[...] = acc[:, KV_LATENT + IDX_DIM:] * IDX_W_SCALE


def _small_proj(h, w_small, kv_g):
    t, d = h.shape
    n = w_small.shape[1]
    tm = 1024
    return pl.pallas_call(
        _small_proj_kernel,
        grid=(t // tm,),
        in_specs=[pl.BlockSpec((tm, d), lambda i: (i, 0)),
                  pl.BlockSpec((d, n), lambda i: (0, 0)),
                  pl.BlockSpec((1, KV_LATENT), lambda i: (0, 0))],
        out_specs=[pl.BlockSpec((tm, KV_LATENT), lambda i: (i, 0)),
                   pl.BlockSpec((tm, IDX_DIM), lambda i: (i, 0)),
                   pl.BlockSpec((tm, LANE), lambda i: (i, 0))],
        out_shape=[jax.ShapeDtypeStruct((t, KV_LATENT), BF16),
                   jax.ShapeDtypeStruct((t, IDX_DIM), BF16),
                   jax.ShapeDtypeStruct((t, LANE), F32)],
        compiler_params=_params("arbitrary"),
        name="small_proj",
    )(h, w_small, kv_g.reshape(1, KV_LATENT))


def _indexer_kernel(q_ref, k_ref, w_ref, bias_ref, key_scr, keyt_scr, ngt_scr, run_scr,
                    *, n_kt, k_sel):
    i = pl.program_id(1)
    q = q_ref[...]
    qs = jnp.concatenate([q[:, h * IDX_DIM:(h + 1) * IDX_DIM] for h in range(N_HEADS)], axis=0)
    w = w_ref[...]
    wcols = [w[:, h:h + 1] for h in range(N_HEADS)]
    row = lax.broadcasted_iota(I32, (Q_BLOCK, KEY_TILE), 0)
    col = lax.broadcasted_iota(I32, (Q_BLOCK, KEY_TILE), 1)
    q_chunk = (i * Q_BLOCK + row) // CHUNK
    n_used = (i * Q_BLOCK) // KEY_TILE + 1

    for j in range(n_kt):
        @pl.when(j < n_used)
        def _():
            k = k_ref[0, j * KEY_TILE:(j + 1) * KEY_TILE, :]
            logits = lax.dot_general(qs, k, (((1,), (1,)), ((), ())),
                                     preferred_element_type=F32)
            sc = wcols[0] * jnp.maximum(logits[0:Q_BLOCK], 0.0)
            for h in range(1, N_HEADS):
                sc = sc + wcols[h] * jnp.maximum(logits[h * Q_BLOCK:(h + 1) * Q_BLOCK], 0.0)
            bits = lax.bitcast_convert_type(sc, I32)
            key = bits ^ ((bits >> 31) & 0x7FFFFFFF)
            allowed = (j * KEY_TILE + col) // CHUNK <= q_chunk
            key = jnp.where(allowed, key, INT_MIN)
            key_scr[j] = key
            keyt_scr[j] = key.T

    def count_tile(hit):
        ones = jnp.where(hit, 1.0, 0.0)
        return jnp.sum(ones.reshape(KEY_TILE // COUNT_ROWS, COUNT_ROWS, Q_BLOCK), axis=0)

    def bisect(it, t_u):
        cand_u = t_u | jnp.left_shift(jnp.int32(1), 31 - it)
        cand_s = cand_u ^ INT_MIN

        def count(j, cnt):
            return cnt + count_tile(keyt_scr[j] >= cand_s)

        cnt = lax.fori_loop(0, n_used, count, jnp.zeros((COUNT_ROWS, Q_BLOCK), F32))
        tot = jnp.sum(cnt, axis=0, keepdims=True)
        return jnp.where(tot >= float(k_sel), cand_u, t_u)

    t_u = lax.fori_loop(0, 32, bisect, jnp.zeros((1, Q_BLOCK), I32))
    t_s = jnp.maximum(t_u ^ INT_MIN, INT_MIN + 1)
    t_col = jnp.broadcast_to(t_s, (Q_BLOCK, Q_BLOCK)).T
    t_tile = jnp.concatenate([t_col] * (KEY_TILE // Q_BLOCK), axis=1)

    def count_ge(j, cnt):
        return cnt + count_tile(keyt_scr[j] >= t_s)

    n_ge = jnp.sum(lax.fori_loop(0, n_used, count_ge, jnp.zeros((COUNT_ROWS, Q_BLOCK), F32)),
                   axis=0, keepdims=True)
    has_ties = jnp.max(n_ge) > float(k_sel)

    def widen(x):
        return jnp.concatenate([x] * (KEY_TILE // LANE), axis=1)

    @pl.when(jnp.logical_not(has_ties))
    def _():
        for j in range(n_kt):
            @pl.when(j < n_used)
            def _():
                bias_ref[0, 0, j] = jnp.where(key_scr[j] >= t_tile, 0.0,
                                              NEG_BIAS).astype(bias_ref.dtype)

    @pl.when(has_ties)
    def _():
        ngt_scr[...] = jnp.zeros(ngt_scr.shape, F32)
        run_scr[...] = jnp.zeros(run_scr.shape, F32)
        for j in range(n_kt):
            @pl.when(j < n_used)
            def _():
                ngt_scr[...] += jnp.sum(jnp.where(key_scr[j] > t_tile, 1.0, 0.0),
                                        axis=-1, keepdims=True)
        a = lax.broadcasted_iota(I32, (KEY_TILE, KEY_TILE), 0)
        b = lax.broadcasted_iota(I32, (KEY_TILE, KEY_TILE), 1)
        before = jnp.where(a < b, 1.0, 0.0).astype(BF16)
        room = widen(float(k_sel) - ngt_scr[...])
        for j in range(n_kt):
            @pl.when(j < n_used)
            def _():
                key = key_scr[j]
                tie = jnp.where(key == t_tile, 1.0, 0.0)
                ahead = jnp.dot(tie.astype(BF16), before, preferred_element_type=F32) \
                    + widen(run_scr[...])
                keep_tie = jnp.where(ahead < room, tie, 0.0)
                keep = jnp.where(key > t_tile, 1.0, keep_tie)
                bias_ref[0, 0, j] = jnp.where(keep > 0.5, 0.0, NEG_BIAS).astype(bias_ref.dtype)
                run_scr[...] += jnp.sum(tie, axis=-1, keepdims=True)

    for j in range(n_kt):
        @pl.when(j >= n_used)
        def _():
            bias_ref[0, 0, j] = jnp.full((Q_BLOCK, KEY_TILE), NEG_BIAS, bias_ref.dtype)


def _indexer(q_idx, k_idx3, w_idx, bsz, seq):
    n_q = seq // Q_BLOCK
    n_kt = seq // KEY_TILE
    k_sel = min(TOPK_KEYS_MAX, seq // 4)
    return pl.pallas_call(
        functools.partial(_indexer_kernel, n_kt=n_kt, k_sel=k_sel),
        grid=(bsz, n_q),
        in_specs=[pl.BlockSpec((Q_BLOCK, N_HEADS * IDX_DIM), lambda b, i: (b * n_q + i, 0)),
                  pl.BlockSpec((1, seq, IDX_DIM), lambda b, i: (b, 0, 0)),
                  pl.BlockSpec((Q_BLOCK, LANE), lambda b, i: (b * n_q + i, 0))],
        out_specs=pl.BlockSpec((1, 1, n_kt, Q_BLOCK, KEY_TILE), lambda b, i: (b, i, 0, 0, 0)),
        out_shape=jax.ShapeDtypeStruct((bsz, n_q, n_kt, Q_BLOCK, KEY_TILE), BF16),
        scratch_shapes=[pltpu.VMEM((n_kt, Q_BLOCK, KEY_TILE), I32),
                        pltpu.VMEM((n_kt, KEY_TILE, Q_BLOCK), I32),
                        pltpu.VMEM((Q_BLOCK, LANE), F32),
                        pltpu.VMEM((Q_BLOCK, LANE), F32)],
        compiler_params=_params("arbitrary", "arbitrary"),
        name="indexer",
    )(q_idx, k_idx3, w_idx)


def _attn_kernel(qa_ref, c_ref, bias_ref, wuk_ref, wuv_ref, o_ref,
                 s_scr, q_scr, m_scr, l_scr, acc_scr):
    i = pl.program_id(1)
    n_used = (i * Q_BLOCK) // KEY_TILE + 1
    rows = N_HEADS * Q_BLOCK

    qa = qa_ref[...]
    for h in range(N_HEADS):
        ql = jnp.dot(qa[:, h * HEAD_DIM:(h + 1) * HEAD_DIM], wuk_ref[h],
                     preferred_element_type=F32)
        q_scr[h * Q_BLOCK:(h + 1) * Q_BLOCK, :] = (ql * (ATTN_SCALE * LOG2_E)).astype(BF16)

    m_scr[...] = jnp.full((rows, LANE), -jnp.inf, F32)

    def scores(j, carry):
        off = pl.multiple_of(j * KEY_TILE, KEY_TILE)
        k = c_ref[0, pl.ds(off, KEY_TILE), :]
        s = lax.dot_general(q_scr[...], k, (((1,), (1,)), ((), ())),
                            preferred_element_type=F32)
        b = bias_ref[0, 0, j].astype(F32)
        for h in range(N_HEADS):
            sl = slice(h * Q_BLOCK, (h + 1) * Q_BLOCK)
            sb = s[sl] + b
            s_scr[j, sl, :] = sb
            m_scr[sl, :] = jnp.maximum(m_scr[sl, :], jnp.maximum(sb[:, :LANE], sb[:, LANE:]))
        return carry

    lax.fori_loop(0, n_used, scores, 0)

    m = jnp.max(m_scr[...], axis=-1, keepdims=True)
    l_scr[...] = jnp.zeros((rows, LANE), F32)
    acc_scr[...] = jnp.zeros((rows, KV_LATENT), F32)

    def values(j, carry):
        off = pl.multiple_of(j * KEY_TILE, KEY_TILE)
        p = jnp.exp2(s_scr[j] - m)
        l_scr[...] += p[:, :LANE] + p[:, LANE:]
        acc_scr[...] += jnp.dot(p.astype(BF16), c_ref[0, pl.ds(off, KEY_TILE), :],
                                preferred_element_type=F32)
        return carry

    lax.fori_loop(0, n_used, values, 0)

    inv_l = 1.0 / jnp.sum(l_scr[...], axis=-1, keepdims=True)
    o = (acc_scr[...] * inv_l).astype(BF16)
    for h in range(N_HEADS):
        o_ref[:, h * HEAD_DIM:(h + 1) * HEAD_DIM] = jnp.dot(
            o[h * Q_BLOCK:(h + 1) * Q_BLOCK], wuv_ref[h],
            preferred_element_type=F32).astype(o_ref.dtype)


def _attention(q_a, c_kv3, bias, w_uk, w_uv, bsz, seq):
    n_q = seq // Q_BLOCK
    n_kt = seq // KEY_TILE
    rows = N_HEADS * Q_BLOCK
    t = bsz * seq
    return pl.pallas_call(
        _attn_kernel,
        grid=(bsz, n_q),
        in_specs=[pl.BlockSpec((Q_BLOCK, N_HEADS * HEAD_DIM), lambda b, i: (b * n_q + i, 0)),
                  pl.BlockSpec((1, seq, KV_LATENT), lambda b, i: (b, 0, 0)),
                  pl.BlockSpec((1, 1, n_kt, Q_BLOCK, KEY_TILE), lambda b, i: (b, i, 0, 0, 0)),
                  pl.BlockSpec((N_HEADS, HEAD_DIM, KV_LATENT), lambda b, i: (0, 0, 0)),
                  pl.BlockSpec((N_HEADS, KV_LATENT, HEAD_DIM), lambda b, i: (0, 0, 0))],
        out_specs=pl.BlockSpec((Q_BLOCK, N_HEADS * HEAD_DIM), lambda b, i: (b * n_q + i, 0)),
        out_shape=jax.ShapeDtypeStruct((t, N_HEADS * HEAD_DIM), BF16),
        scratch_shapes=[pltpu.VMEM((n_kt, rows, KEY_TILE), F32),
                        pltpu.VMEM((rows, KV_LATENT), BF16),
                        pltpu.VMEM((rows, LANE), F32),
                        pltpu.VMEM((rows, LANE), F32),
                        pltpu.VMEM((rows, KV_LATENT), F32)],
        compiler_params=_params("arbitrary", "arbitrary"),
        name="attention",
    )(q_a, c_kv3, bias, w_uk, w_uv)


def _spatial_kernel(z_ref, g_ref, ws_ref, bs_ref, o_ref, *, width):
    z = z_ref[...].astype(F32)
    z = 0.5 * z * (1.0 + lax.erf(z * (2.0 ** -0.5)))
    u = z[:, :width]
    v = z[:, width:]
    mu = jnp.mean(v, axis=-1, keepdims=True)
    vc = v - mu
    var = jnp.mean(vc * vc, axis=-1, keepdims=True)
    vn = (vc * lax.rsqrt(var + EPS) * g_ref[...]).astype(BF16)
    r = lax.broadcasted_iota(I32, (SG_CHUNK, SG_CHUNK), 0)
    c = lax.broadcasted_iota(I32, (SG_CHUNK, SG_CHUNK), 1)
    gd = width // N_GROUPS_B
    bs = bs_ref[...]
    for g in range(N_GROUPS_B):
        wg = jnp.where(r >= c, ws_ref[g], 0.0).astype(BF16)
        s = jnp.dot(wg, vn[:, g * gd:(g + 1) * gd], preferred_element_type=F32) + bs[:, g:g + 1]
        o_ref[:, g * gd:(g + 1) * gd] = (u[:, g * gd:(g + 1) * gd] * s).astype(o_ref.dtype)


def _spatial_gating(z, sg_g, w_spatial, b_spatial_t):
    t, w2 = z.shape
    width = w2 // 2
    return pl.pallas_call(
        functools.partial(_spatial_kernel, width=width),
        grid=(t // SG_CHUNK,),
        in_specs=[pl.BlockSpec((SG_CHUNK, w2), lambda i: (i, 0)),
                  pl.BlockSpec((1, width), lambda i: (0, 0)),
                  pl.BlockSpec((N_GROUPS_B, SG_CHUNK, SG_CHUNK), lambda i: (0, 0, 0)),
                  pl.BlockSpec((SG_CHUNK, LANE), lambda i: (0, 0))],
        out_specs=pl.BlockSpec((SG_CHUNK, width), lambda i: (i, 0)),
        out_shape=jax.ShapeDtypeStruct((t, width), BF16),
        compiler_params=_params("arbitrary"),
        name="spatial_gating",
    )(z, sg_g.reshape(1, width), w_spatial, b_spatial_t)


def _mix_kernel(a_ref, b_ref, wa_ref, wb_ref, ga_ref, gb_ref, o_ref):
    ya = jnp.dot(a_ref[...], wa_ref[...], preferred_element_type=F32)
    yb = jnp.dot(b_ref[...], wb_ref[...], preferred_element_type=F32)
    o_ref[...] = (ga_ref[...].astype(F32) * ya + gb_ref[...].astype(F32) * yb).astype(o_ref.dtype)


def _mix(o_a, sg, w_a, w_b, gates):
    t, k = o_a.shape
    d = w_a.shape[1]
    tm, tn = 1024, 1024
    nb = d // tn
    return pl.pallas_call(
        _mix_kernel,
        grid=(nb, t // tm),
        in_specs=[pl.BlockSpec((tm, k), lambda j, i: (i, 0)),
                  pl.BlockSpec((tm, k), lambda j, i: (i, 0)),
                  pl.BlockSpec((k, tn), lambda j, i: (0, j)),
                  pl.BlockSpec((k, tn), lambda j, i: (0, j)),
                  pl.BlockSpec((tm, tn), lambda j, i: (i, j)),
                  pl.BlockSpec((tm, tn), lambda j, i: (i, nb + j))],
        out_specs=pl.BlockSpec((tm, tn), lambda j, i: (i, j)),
        out_shape=jax.ShapeDtypeStruct((t, d), BF16),
        compiler_params=_params("arbitrary", "arbitrary"),
        name="branch_mix",
    )(o_a, sg, w_a, w_b, gates, gates)


def _out_router_kernel(a_ref, w_ref, x_ref, mod_ref, g_ref, wr_ref, br_ref,
                       x1_ref, h2_ref, lg_ref):
    mix = jnp.dot(a_ref[...], w_ref[...], preferred_element_type=F32)
    x1 = x_ref[...] + mod_ref[0, 2:3, :] * mix
    x1_ref[...] = x1
    y = x1 * lax.rsqrt(jnp.mean(x1 * x1, axis=-1, keepdims=True) + EPS) * g_ref[...]
    h2 = y * (1.0 + mod_ref[0, 4:5, :]) + mod_ref[0, 3:4, :]
    h2_ref[...] = h2
    h_hi = h2.astype(BF16)
    h_lo = (h2 - h_hi.astype(F32)).astype(BF16)
    wr = wr_ref[...]
    w_hi = wr.astype(BF16)
    w_lo = (wr - w_hi.astype(F32)).astype(BF16)
    lg = jnp.dot(h_hi, w_hi, preferred_element_type=F32)
    lg = lg + jnp.dot(h_lo, w_hi, preferred_element_type=F32)
    lg = lg + jnp.dot(h_hi, w_lo, preferred_element_type=F32)
    lg_ref[...] = lg + br_ref[...]


def _out_router(mixpre, w_out, x2, mod6, norm2_g, w_router_p, b_router_p, seq):
    t, d = x2.shape
    tm = 256
    return pl.pallas_call(
        _out_router_kernel,
        grid=(t // tm,),
        in_specs=[pl.BlockSpec((tm, d), lambda i: (i, 0)),
                  pl.BlockSpec((d, d), lambda i: (0, 0)),
                  pl.BlockSpec((tm, d), lambda i: (i, 0)),
                  pl.BlockSpec((1, 6, d), lambda i: ((i * tm) // seq, 0, 0)),
                  pl.BlockSpec((1, d), lambda i: (0, 0)),
                  pl.BlockSpec((d, LANE), lambda i: (0, 0)),
                  pl.BlockSpec((1, LANE), lambda i: (0, 0))],
        out_specs=[pl.BlockSpec((tm, d), lambda i: (i, 0)),
                   pl.BlockSpec((tm, d), lambda i: (i, 0)),
                   pl.BlockSpec((tm, LANE), lambda i: (i, 0))],
        out_shape=[jax.ShapeDtypeStruct((t, d), F32),
                   jax.ShapeDtypeStruct((t, d), F32),
                   jax.ShapeDtypeStruct((t, LANE), F32)],
        compiler_params=_params("arbitrary"),
        name="out_proj_router",
    )(mixpre, w_out, x2, mod6, norm2_g.reshape(1, d), w_router_p, b_router_p)


DISPATCH_TOKENS = 512


def _pack_bf16_pairs(x):
    half = x.shape[1] // 2
    hi = lax.bitcast_convert_type(x[:, :half].astype(BF16).astype(F32), I32)
    lo = lax.bitcast_convert_type(x[:, half:].astype(BF16).astype(F32), I32)
    return hi | lax.shift_right_logical(lo, 16)


def _unpack_bf16_pairs(p):
    hi = lax.bitcast_convert_type(p & jnp.int32(-65536), F32).astype(BF16)
    lo = lax.bitcast_convert_type(lax.shift_left(p, 16), F32).astype(BF16)
    return hi, lo


def _dispatch_kernel(pos_ref, pad_ref, h_ref, xs_hbm, pk_scr, zero_scr, sem, *, n_alloc_blocks):
    i = pl.program_id(0)
    n_tiles = pl.num_programs(0)
    packed = _pack_bf16_pairs(h_ref[...])
    for c in range(pk_scr.shape[1]):
        pk_scr[:, c, :] = packed[:, c * LANE:(c + 1) * LANE]
    zero_scr[...] = jnp.zeros(zero_scr.shape, zero_scr.dtype)

    def row_copy(t, k):
        return pltpu.make_async_copy(
            pk_scr.at[t], xs_hbm.at[pos_ref[0, 0, t * TOP_K + k]], sem)

    def pad_copy(r):
        return pltpu.make_async_copy(zero_scr.at[0], xs_hbm.at[r], sem)

    def tail_copy(b):
        off = pl.multiple_of(b * MOE_ROWS, MOE_ROWS)
        return pltpu.make_async_copy(zero_scr, xs_hbm.at[pl.ds(off, MOE_ROWS)], sem)

    e_lo = (i * N_EXPERTS) // n_tiles
    e_hi = ((i + 1) * N_EXPERTS) // n_tiles
    tail_lo = jnp.where(i == 0, pad_ref[2, 0], n_alloc_blocks)

    def for_rows(fn):
        def per_token(t, carry):
            for k in range(TOP_K):
                fn(row_copy(t, k), k)
            return carry

        def per_expert(e, carry):
            def per_pad(r, c):
                fn(pad_copy(r), 0)
                return c
            return lax.fori_loop(pad_ref[0, e], pad_ref[1, e], per_pad, carry)

        def per_tail(b, carry):
            fn(tail_copy(b), 0)
            return carry

        lax.fori_loop(0, DISPATCH_TOKENS, per_token, 0)
        lax.fori_loop(e_lo, e_hi, per_expert, 0)
        lax.fori_loop(tail_lo, n_alloc_blocks, per_tail, 0)

    for_rows(lambda cp, k: cp.start(priority=k % 2))
    for_rows(lambda cp, k: cp.wait())


def _dispatch(pos, pad_rng, h2, n_alloc_rows):
    t, d = h2.shape
    n_tiles = t // DISPATCH_TOKENS
    pos3 = pos.reshape(n_tiles, 1, DISPATCH_TOKENS * TOP_K)
    return pl.pallas_call(
        functools.partial(_dispatch_kernel, n_alloc_blocks=n_alloc_rows // MOE_ROWS),
        grid=(n_tiles,),
        in_specs=[pl.BlockSpec((1, 1, DISPATCH_TOKENS * TOP_K), lambda i: (i, 0, 0),
                               memory_space=pltpu.SMEM),
                  pl.BlockSpec(memory_space=pltpu.SMEM),
                  pl.BlockSpec((DISPATCH_TOKENS, d), lambda i: (i, 0))],
        out_specs=pl.BlockSpec(memory_space=pl.ANY),
        out_shape=jax.ShapeDtypeStruct((n_alloc_rows, d // (2 * LANE), LANE), I32),
        scratch_shapes=[pltpu.VMEM((DISPATCH_TOKENS, d // (2 * LANE), LANE), I32),
                        pltpu.VMEM((MOE_ROWS, d // (2 * LANE), LANE), I32),
                        pltpu.SemaphoreType.DMA(())],
        compiler_params=_params("arbitrary"),
        name="moe_dispatch",
    )(pos3, pad_rng, h2)


def _ffn_kernel(ie_ref, in_ref, ix_ref, io_ref, no_ref, xs_hbm, wgu_hbm, wdn_hbm, bg_ref, bl_ref,
                bd_ref, o_hbm, x_scr, xw_scr, acc_scr, wg_buf, wl_buf, wd_buf, sem, sem_x, sem_w):
    i = pl.program_id(0)
    f = pl.program_id(1)
    n_items = pl.num_programs(0)
    n_f = pl.num_programs(1)
    n_sub = in_ref[i]
    half = x_scr.shape[1] // 2
    r = MOE_ROWS
    win = xw_scr.shape[1]
    tf = wg_buf.shape[2]
    d_ff = wdn_hbm.shape[1]
    o_ref = acc_scr.at[i % 2]
    step = i * n_f + f
    ahead = WEIGHT_SLOTS - 1

    def for_weights(g, fn):
        item = g // n_f
        ft = g - item * n_f
        e = ie_ref[item]
        slot = g % WEIGHT_SLOTS
        col = pl.multiple_of(ft * tf, tf)
        fn(pltpu.make_async_copy(wgu_hbm.at[e, :, pl.ds(col, tf)], wg_buf.at[slot],
                                 sem_w.at[slot]))
        fn(pltpu.make_async_copy(wgu_hbm.at[e, :, pl.ds(d_ff + col, tf)], wl_buf.at[slot],
                                 sem_w.at[slot]))
        fn(pltpu.make_async_copy(wdn_hbm.at[e, pl.ds(col, tf), :], wd_buf.at[slot],
                                 sem_w.at[slot]))

    def fetch_if_used(g):
        @pl.when(g < n_items * n_f)
        def _():
            @pl.when(in_ref[g // n_f] > 0)
            def _():
                for_weights(g, lambda cp: cp.start())

    @pl.when(step == 0)
    def _():
        for g in range(ahead):
            fetch_if_used(g)

    fetch_if_used(step + ahead)

    @pl.when(n_sub > 0)
    def _():
        for_weights(step, lambda cp: cp.wait())

    w_slot = step % WEIGHT_SLOTS

    def for_window(item, fn):
        off = pl.multiple_of(ix_ref[item] * r, r)
        for c in range(half // LANE):
            fn(pltpu.make_async_copy(xs_hbm.at[pl.ds(off, win), c, :],
                                     xw_scr.at[item % 2, :, pl.ds(c * LANE, LANE)],
                                     sem_x.at[item % 2]))

    def out_copy(item, s):
        off = pl.multiple_of(io_ref[item] * r, r)
        return pltpu.make_async_copy(acc_scr.at[item % 2, pl.ds(0, s * r), pl.ds(0, half)],
                                     o_hbm.at[pl.ds(off, s * r), :], sem.at[item % 2])

    def for_out(item, fn):
        for s in range(1, ITEM_SUB + 1):
            @pl.when(no_ref[item] == s)
            def _():
                fn(out_copy(item, s))

    @pl.when(f == 0)
    def _():
        @pl.when(i == 0)
        def _():
            for_window(0, lambda cp: cp.start())

        @pl.when(i + 1 < n_items)
        def _():
            for_window(i + 1, lambda cp: cp.start())

        for_window(i, lambda cp: cp.wait())

        @pl.when(i >= 2)
        def _():
            for_out(i - 2, lambda cp: cp.wait())
        o_ref[...] = jnp.broadcast_to(bd_ref[0], o_ref.shape)

    def unpack(rows):
        hi, lo = _unpack_bf16_pairs(xw_scr[i % 2, :rows, :])
        x_scr[:rows, :half] = hi
        x_scr[:rows, half:] = lo

    def ffn(xb):
        g = jnp.dot(xb, wg_buf[w_slot].astype(BF16), preferred_element_type=F32) + bg_ref[0]
        lin = jnp.dot(xb, wl_buf[w_slot].astype(BF16), preferred_element_type=F32) + bl_ref[0]
        g = jnp.minimum(g, SWIGLU_LIMIT)
        lin = jnp.clip(lin, -SWIGLU_LIMIT, SWIGLU_LIMIT)
        act = g * jax.nn.sigmoid(SWIGLU_ALPHA * g) * (lin + 1.0)
        return jnp.dot(act.astype(BF16), wd_buf[w_slot].astype(BF16), preferred_element_type=F32)

    for s in range(1, ITEM_SUB + 1):
        rows = s * r

        @pl.when(jnp.logical_and(n_sub == s, f == 0))
        def _():
            unpack(rows)

        @pl.when(n_sub == s)
        def _():
            o_ref[:rows, :] += ffn(x_scr[:rows, :])

    @pl.when(f == n_f - 1)
    def _():
        o_ref[:, :half] = lax.bitcast_convert_type(_pack_bf16_pairs(o_ref[...]), F32)
        for_out(i, lambda cp: cp.start())

        @pl.when(i == n_items - 1)
        def _():
            @pl.when(i >= 1)
            def _():
                for_out(i - 1, lambda cp: cp.wait())
            for_out(i, lambda cp: cp.wait())


def _expert_ffn(items, xs, w_gate_up, b_gate_up3, w_down, b_down3):
    p = xs.shape[0]
    dh = xs.shape[1] * xs.shape[2]
    d = 2 * dh
    n_e, _, f2 = w_gate_up.shape
    d_ff = f2 // 2
    tf = FFN_TILE
    nf = d_ff // tf
    n_items = items[0].shape[0]
    win = ITEM_SUB * MOE_ROWS

    def ft(f, ni, i):
        return jnp.where(ni[i] > 0, f, nf - 1)

    grid_spec = pltpu.PrefetchScalarGridSpec(
        num_scalar_prefetch=5,
        grid=(n_items, nf),
        in_specs=[pl.BlockSpec(memory_space=pl.ANY),
                  pl.BlockSpec(memory_space=pl.ANY),
                  pl.BlockSpec(memory_space=pl.ANY),
                  pl.BlockSpec((1, 1, tf), lambda i, f, ie, ni, ix, io, no: (ie[i], 0, ft(f, ni, i))),
                  pl.BlockSpec((1, 1, tf),
                               lambda i, f, ie, ni, ix, io, no: (ie[i], 0, nf + ft(f, ni, i))),
                  pl.BlockSpec((1, 1, d), lambda i, f, ie, ni, ix, io, no: (ie[i], 0, 0))],
        out_specs=pl.BlockSpec(memory_space=pl.ANY),
        scratch_shapes=[pltpu.VMEM((win, d), BF16),
                        pltpu.VMEM((2, win, dh), I32),
                        pltpu.VMEM((2, win, d), F32),
                        pltpu.VMEM((WEIGHT_SLOTS, d, tf), F32),
                        pltpu.VMEM((WEIGHT_SLOTS, d, tf), F32),
                        pltpu.VMEM((WEIGHT_SLOTS, tf, d), F32),
                        pltpu.SemaphoreType.DMA((2,)),
                        pltpu.SemaphoreType.DMA((2,)),
                        pltpu.SemaphoreType.DMA((WEIGHT_SLOTS,))],
    )
    return pl.pallas_call(
        _ffn_kernel,
        grid_spec=grid_spec,
        out_shape=jax.ShapeDtypeStruct((p, dh), F32),
        compiler_params=_params("arbitrary", "arbitrary"),
        name="expert_ffn",
    )(*items, xs, w_gate_up, w_down, b_gate_up3, b_gate_up3, b_down3)


COMBINE_TOKENS = 256


def _combine_kernel(pos_ref, pos_next_ref, ys_hbm, gate_ref, x1_ref, mod_ref, g_ref, o_ref,
                    buf, sem):
    i = pl.program_id(0)
    n = pl.num_programs(0)
    slot = i % 2

    def issue(p_ref, sl):
        def body(t, carry):
            for k in range(TOP_K):
                pltpu.make_async_copy(ys_hbm.at[pl.ds(p_ref[0, 0, t * TOP_K + k], 1), :],
                                      buf.at[sl, k, pl.ds(t, 1), :],
                                      sem.at[sl]).start(priority=k % 2)
            return carry
        lax.fori_loop(0, COMBINE_TOKENS, body, 0)

    def wait_all(sl):
        def body(t, carry):
            for k in range(TOP_K):
                pltpu.make_async_copy(ys_hbm.at[pl.ds(0, 1), :],
                                      buf.at[sl, k, pl.ds(t, 1), :], sem.at[sl]).wait()
            return carry
        lax.fori_loop(0, COMBINE_TOKENS, body, 0)

    @pl.when(i == 0)
    def _():
        issue(pos_ref, 0)

    @pl.when(i + 1 < n)
    def _():
        issue(pos_next_ref, 1 - slot)

    wait_all(slot)

    gate = gate_ref[...]
    d = x1_ref.shape[1]
    half = d // 2
    moe_hi = jnp.zeros((COMBINE_TOKENS, half), F32)
    moe_lo = jnp.zeros((COMBINE_TOKENS, half), F32)
    for k in range(TOP_K):
        bits = lax.bitcast_convert_type(buf[slot, k], I32)
        hi = lax.bitcast_convert_type(bits & jnp.int32(-65536), F32)
        lo = lax.bitcast_convert_type(lax.shift_left(bits, 16), F32)
        moe_hi = moe_hi + gate[:, k:k + 1] * hi
        moe_lo = moe_lo + gate[:, k:k + 1] * lo
    x_hi = x1_ref[:, :half] + mod_ref[0, 5:6, :half] * moe_hi
    x_lo = x1_ref[:, half:] + mod_ref[0, 5:6, half:] * moe_lo
    ss = jnp.sum(x_hi * x_hi, axis=-1, keepdims=True) + jnp.sum(x_lo * x_lo, axis=-1, keepdims=True)
    inv = lax.rsqrt(ss * (1.0 / d) + EPS)
    o_ref[:, :half] = x_hi * inv * g_ref[:, :half]
    o_ref[:, half:] = x_lo * inv * g_ref[:, half:]


def _combine(pos, ys, gates_p, x1, mod6, final_g, seq):
    t, d = x1.shape
    tt = COMBINE_TOKENS
    n_tiles = t // tt
    pos3 = pos.reshape(n_tiles, 1, tt * TOP_K)
    return pl.pallas_call(
        _combine_kernel,
        grid=(n_tiles,),
        in_specs=[pl.BlockSpec((1, 1, tt * TOP_K), lambda i: (i, 0, 0), memory_space=pltpu.SMEM),
                  pl.BlockSpec((1, 1, tt * TOP_K), lambda i: (jnp.minimum(i + 1, n_tiles - 1), 0, 0),
                               memory_space=pltpu.SMEM),
                  pl.BlockSpec(memory_space=pl.ANY),
                  pl.BlockSpec((tt, LANE), lambda i: (i, 0)),
                  pl.BlockSpec((tt, d), lambda i: (i, 0)),
                  pl.BlockSpec((1, 6, d), lambda i: ((i * tt) // seq, 0, 0)),
                  pl.BlockSpec((1, d), lambda i: (0, 0))],
        out_specs=pl.BlockSpec((tt, d), lambda i: (i, 0)),
        out_shape=jax.ShapeDtypeStruct((t, d), F32),
        scratch_shapes=[pltpu.VMEM((2, TOP_K, tt, d // 2), F32),
                        pltpu.SemaphoreType.DMA((2,))],
        compiler_params=_params("arbitrary"),
        name="moe_combine",
    )(pos3, pos3, ys, gates_p, x1, mod6, final_g.reshape(1, d))


ROUTE_TOKENS = 512


def _route_kernel(lg_ref, idx_ref, gate_ref, rank_ref, cnt_ref, carry_scr):
    i = pl.program_id(0)
    tt = lg_ref.shape[0]

    @pl.when(i == 0)
    def _():
        carry_scr[...] = jnp.zeros(carry_scr.shape, F32)

    lane = lax.broadcasted_iota(I32, (tt, LANE), 1)
    v = jnp.where(lane < N_EXPERTS, lg_ref[...], -jnp.inf)
    tops, ids, hits = [], [], []
    for _ in range(TOP_K):
        m = jnp.max(v, axis=-1, keepdims=True)
        idx = jnp.min(jnp.where(v == m, lane, LANE), axis=-1, keepdims=True)
        hit = lane == idx
        v = jnp.where(hit, -jnp.inf, v)
        tops.append(m)
        ids.append(idx)
        hits.append(hit)

    exps = [jnp.exp(m - tops[0]) for m in tops]
    total = exps[0]
    for e in exps[1:]:
        total = total + e

    sel = jnp.where(hits[0], 1.0, 0.0)
    for hit in hits[1:]:
        sel = sel + jnp.where(hit, 1.0, 0.0)
    r = lax.broadcasted_iota(I32, (tt, tt), 0)
    c = lax.broadcasted_iota(I32, (tt, tt), 1)
    earlier = jnp.where(c < r, 1.0, 0.0).astype(BF16)
    rank_all = jnp.dot(earlier, sel.astype(BF16), preferred_element_type=F32) + carry_scr[...]
    carry_scr[...] += jnp.sum(sel, axis=0, keepdims=True)
    cnt_ref[...] = carry_scr[...]

    idx_out = jnp.zeros((tt, LANE), I32)
    gate_out = jnp.zeros((tt, LANE), F32)
    rank_out = jnp.zeros((tt, LANE), F32)
    for k in range(TOP_K):
        rk = jnp.sum(jnp.where(hits[k], rank_all, 0.0), axis=-1, keepdims=True)
        idx_out = jnp.where(lane == k, ids[k], idx_out)
        gate_out = jnp.where(lane == k, exps[k] / total, gate_out)
        rank_out = jnp.where(lane == k, rk, rank_out)
    idx_ref[...] = idx_out
    gate_ref[...] = gate_out
    rank_ref[...] = rank_out.astype(I32)


def _route(logits_p):
    t = logits_p.shape[0]
    tt = ROUTE_TOKENS
    return pl.pallas_call(
        _route_kernel,
        grid=(t // tt,),
        in_specs=[pl.BlockSpec((tt, LANE), lambda i: (i, 0))],
        out_specs=[pl.BlockSpec((tt, LANE), lambda i: (i, 0)),
                   pl.BlockSpec((tt, LANE), lambda i: (i, 0)),
                   pl.BlockSpec((tt, LANE), lambda i: (i, 0)),
                   pl.BlockSpec((1, LANE), lambda i: (0, 0))],
        out_shape=[jax.ShapeDtypeStruct((t, LANE), I32),
                   jax.ShapeDtypeStruct((t, LANE), F32),
                   jax.ShapeDtypeStruct((t, LANE), I32),
                   jax.ShapeDtypeStruct((1, LANE), F32)],
        scratch_shapes=[pltpu.VMEM((1, LANE), F32)],
        compiler_params=_params("arbitrary"),
        name="moe_route",
    )(logits_p)


def _routing(logits_p):
    t = logits_p.shape[0]
    idx_p, gates_p, rank_p, cnt = _route(logits_p)
    top_idx = idx_p[:, :TOP_K]
    counts = cnt[0, :N_EXPERTS].astype(I32)
    padded = ((counts + MOE_ROWS - 1) // MOE_ROWS) * MOE_ROWS
    pad_ends = jnp.cumsum(padded)
    pstarts = pad_ends - padded
    pos = (pstarts[top_idx] + rank_p[:, :TOP_K]).astype(I32)

    win = ITEM_SUB * MOE_ROWS
    n_rows = t * TOP_K + N_EXPERTS * MOE_ROWS
    n_alloc = n_rows + win
    n_items = (n_rows // MOE_ROWS + N_EXPERTS * (ITEM_SUB - 1)) // ITEM_SUB + 1

    tail0 = pad_ends[-1]
    pad_rng = jnp.stack([pstarts + counts, pad_ends,
                         jnp.full((N_EXPERTS,), tail0 // MOE_ROWS)]).astype(I32)

    nb_e = padded // MOE_ROWS
    items_e = (nb_e + ITEM_SUB - 1) // ITEM_SUB
    item_ends = jnp.cumsum(items_e)
    item_starts = item_ends - items_e
    n_used = item_ends[-1]
    q = jnp.arange(n_items, dtype=I32)
    e_q = jnp.clip(jnp.searchsorted(item_ends, q, side="right"), 0, N_EXPERTS - 1)
    local = q - item_starts[e_q]
    used = q < n_used
    item_n = jnp.where(used, jnp.minimum(ITEM_SUB, nb_e[e_q] - ITEM_SUB * local), 0)
    item_x = jnp.where(used, pstarts[e_q] // MOE_ROWS + local * ITEM_SUB, 0)
    n_alloc_blocks = n_alloc // MOE_ROWS
    tail_blk = tail0 // MOE_ROWS + (q - n_used) * ITEM_SUB
    item_o = jnp.where(used, item_x, jnp.minimum(tail_blk, n_alloc_blocks - 1))
    item_no = jnp.where(used, item_n, jnp.clip(n_alloc_blocks - tail_blk, 0, ITEM_SUB))
    e_last = jnp.max(jnp.where(items_e > 0, jnp.arange(N_EXPERTS, dtype=I32), 0))
    item_e = jnp.where(used, e_q, e_last)
    items = tuple(a.astype(I32) for a in (item_e, item_n, item_x, item_o, item_no))
    return pos, gates_p, pad_rng, items, n_alloc


def _layer(x2, c_act_in, bsz, seq, w_mod, b_mod, norm1_g, w_in, kv_norm_g, w_uk, w_uv,
           w_proj_a, sg_norm_g, w_spatial, b_spatial, w_proj_b, w_out, norm2_g,
           w_router, b_router, w_gate_up, b_gate_up, w_down, b_down):
    t, d = x2.shape
    w_qa = N_HEADS * HEAD_DIM
    w_qi = N_HEADS * IDX_DIM
    width_b = sg_norm_g.shape[0]

    mod = _modulation(c_act_in, w_mod, b_mod)
    mod6 = mod.reshape(bsz, 6, d)
    h = _norm_mod(x2, norm1_g, mod6[:, :3], seq)

    o0 = w_qa
    o1 = o0 + KV_LATENT
    o2 = o1 + w_qi
    o3 = o2 + IDX_DIM
    o4 = o3 + N_HEADS
    o5 = o4 + 2 * width_b
    w_small = jnp.concatenate(
        [w_in[:, o0:o1], w_in[:, o2:o3], w_in[:, o3:o4],
         jnp.zeros((d, LANE - N_HEADS), w_in.dtype)], axis=1).astype(BF16)

    q_a = _matmul(h, w_in[:, :o0].astype(BF16), name="proj_q_a")
    q_idx = _matmul(h, w_in[:, o1:o2].astype(BF16), name="proj_q_idx")
    z_b = _matmul(h, w_in[:, o4:o5].astype(BF16), name="proj_z_b")
    gates = _matmul(h, w_in[:, o5:].astype(BF16), act="sigmoid", name="proj_gates")
    c_kv, k_idx, w_idx = _small_proj(h, w_small, kv_norm_g)

    bias = _indexer(q_idx, k_idx.reshape(bsz, seq, IDX_DIM), w_idx, bsz, seq)
    o_a = _attention(q_a, c_kv.reshape(bsz, seq, KV_LATENT), bias,
                     w_uk.astype(BF16), w_uv.astype(BF16), bsz, seq)

    b_sp_t = jnp.pad(jnp.transpose(b_spatial), ((0, 0), (0, LANE - N_GROUPS_B)))
    sg = _spatial_gating(z_b, sg_norm_g, w_spatial, b_sp_t)

    mixpre = _mix(o_a, sg, w_proj_a.astype(BF16), w_proj_b.astype(BF16), gates)

    w_router_p = jnp.pad(w_router, ((0, 0), (0, LANE - N_EXPERTS)))
    b_router_p = jnp.pad(b_router, (0, LANE - N_EXPERTS)).reshape(1, LANE)
    x1, h2, logits_p = _out_router(mixpre, w_out.astype(BF16), x2, mod6, norm2_g,
                                   w_router_p, b_router_p, seq)

    pos, gates_p, pad_rng, items, n_alloc = _routing(logits_p)
    xs = _dispatch(pos.reshape(-1), pad_rng, h2, n_alloc)
    n_e = w_gate_up.shape[0]
    ys = _expert_ffn(items, xs, w_gate_up, b_gate_up.reshape(n_e, 1, -1),
                     w_down, b_down.reshape(n_e, 1, -1))
    return pos, ys, gates_p, x1, mod6


def kernel(x, c, w_mod, b_mod, norm1_g, w_in, kv_norm_g, w_uk, w_uv, w_proj_a, sg_norm_g,
           w_spatial, b_spatial, w_proj_b, w_out, norm2_g, w_router, b_router, w_gate_up,
           b_gate_up, w_down, b_down, final_g):
    bsz, seq, d = x.shape
    depth = w_mod.shape[0]
    assert depth == 1, "single-layer block"
    x2 = x.reshape(bsz * seq, d)
    pos, ys, gates_p, x1, mod6 = _layer(
        x2, c, bsz, seq, w_mod[0], b_mod[0], norm1_g[0], w_in[0], kv_norm_g[0], w_uk[0],
        w_uv[0], w_proj_a[0], sg_norm_g[0], w_spatial[0], b_spatial[0], w_proj_b[0],
        w_out[0], norm2_g[0], w_router[0], b_router[0], w_gate_up[0], b_gate_up[0],
        w_down[0], b_down[0])
    out = _combine(pos.reshape(-1), ys, gates_p, x1, mod6, final_g, seq)
    return out.reshape(bsz, seq, d)
```

```python
import functools

import jax
import jax.numpy as jnp
from jax import lax
from jax.experimental import pallas as pl
from jax.experimental.pallas import tpu as pltpu

F32 = jnp.float32
BF16 = jnp.bfloat16
I32 = jnp.int32

EPS = 1e-6
CHUNK = 64
N_HEADS = 16
HEAD_DIM = 128
KV_LATENT = 256
IDX_DIM = 128
TOPK_KEYS_MAX = 256
Q_BLOCK = 128
KEY_TILE = 256
N_GROUPS_B = 8
SG_CHUNK = 128
N_EXPERTS = 32
TOP_K = 4
SWIGLU_ALPHA = 1.702
SWIGLU_LIMIT = 7.0
ATTN_SCALE = HEAD_DIM ** -0.5
LOG2_E = 1.4426950408889634
IDX_W_SCALE = (N_HEADS ** -0.5) * (IDX_DIM ** -0.5)

LANE = 128
COUNT_ROWS = 64
MOE_ROWS = 512
ITEM_SUB = 2
FFN_TILE = 256
WEIGHT_SLOTS = 3
NEG_BIAS = -1e30
INT_MIN = -2147483648
VMEM_LIMIT = 56 * 1024 * 1024


def _params(*sem):
    return pltpu.CompilerParams(dimension_semantics=sem, vmem_limit_bytes=VMEM_LIMIT)


def _mod_kernel(c_ref, w_ref, b_ref, o_ref):
    c = c_ref[...]
    ca = (c * jax.nn.sigmoid(c)).astype(BF16)
    o_ref[...] = jnp.dot(ca, w_ref[...].astype(BF16), preferred_element_type=F32) + b_ref[...]


def _modulation(c, w_mod, b_mod):
    bsz, d = c.shape
    n = w_mod.shape[1]
    tn = 1024
    return pl.pallas_call(
        _mod_kernel,
        grid=(n // tn,),
        in_specs=[pl.BlockSpec((bsz, d), lambda j: (0, 0)),
                  pl.BlockSpec((d, tn), lambda j: (0, j)),
                  pl.BlockSpec((1, tn), lambda j: (0, j))],
        out_specs=pl.BlockSpec((bsz, tn), lambda j: (0, j)),
        out_shape=jax.ShapeDtypeStruct((bsz, n), F32),
        compiler_params=_params("arbitrary"),
        name="modulation",
    )(c, w_mod, b_mod.reshape(1, n))


def _norm_mod_kernel(x_ref, g_ref, mod_ref, o_ref):
    x = x_ref[...]
    y = x * lax.rsqrt(jnp.mean(x * x, axis=-1, keepdims=True) + EPS) * g_ref[...]
    o_ref[...] = (y * (1.0 + mod_ref[0, 1:2, :]) + mod_ref[0, 0:1, :]).astype(o_ref.dtype)


def _norm_mod(x2, g, mod3, seq):
    t, d = x2.shape
    tm = 512
    return pl.pallas_call(
        _norm_mod_kernel,
        grid=(t // tm,),
        in_specs=[pl.BlockSpec((tm, d), lambda i: (i, 0)),
                  pl.BlockSpec((1, d), lambda i: (0, 0)),
                  pl.BlockSpec((1, 3, d), lambda i: ((i * tm) // seq, 0, 0))],
        out_specs=pl.BlockSpec((tm, d), lambda i: (i, 0)),
        out_shape=jax.ShapeDtypeStruct((t, d), BF16),
        compiler_params=_params("arbitrary"),
        name="norm_mod",
    )(x2, g.reshape(1, d), mod3)


def _mm_kernel(a_ref, w_ref, o_ref, *, act):
    acc = jnp.dot(a_ref[...], w_ref[...], preferred_element_type=F32)
    if act == "sigmoid":
        acc = jax.nn.sigmoid(acc)
    o_ref[...] = acc.astype(o_ref.dtype)


def _matmul(a, w, *, act=None, out_dtype=BF16, tm=1024, tn=1024, name="matmul"):
    m, k = a.shape
    n = w.shape[1]
    return pl.pallas_call(
        functools.partial(_mm_kernel, act=act),
        grid=(n // tn, m // tm),
        in_specs=[pl.BlockSpec((tm, k), lambda j, i: (i, 0)),
                  pl.BlockSpec((k, tn), lambda j, i: (0, j))],
        out_specs=pl.BlockSpec((tm, tn), lambda j, i: (i, j)),
        out_shape=jax.ShapeDtypeStruct((m, n), out_dtype),
        compiler_params=_params("arbitrary", "arbitrary"),
        name=name,
    )(a, w)


def _small_proj_kernel(a_ref, w_ref, g_ref, ckv_ref, kidx_ref, widx_ref):
    acc = jnp.dot(a_ref[...], w_ref[...], preferred_element_type=F32)
    c = acc[:, :KV_LATENT]
    cn = c * lax.rsqrt(jnp.mean(c * c, axis=-1, keepdims=True) + EPS) * g_ref[...]
    ckv_ref[...] = cn.astype(ckv_ref.dtype)
    kidx_ref[...] = acc[:, KV_LATENT:KV_LATENT + IDX_DIM].astype(kidx_ref.dtype)
    widx_ref[...] = acc[:, KV_LATENT + IDX_DIM:] * IDX_W_SCALE


def _small_proj(h, w_small, kv_g):
    t, d = h.shape
    n = w_small.shape[1]
    tm = 1024
    return pl.pallas_call(
        _small_proj_kernel,
        grid=(t // tm,),
        in_specs=[pl.BlockSpec((tm, d), lambda i: (i, 0)),
                  pl.BlockSpec((d, n), lambda i: (0, 0)),
                  pl.BlockSpec((1, KV_LATENT), lambda i: (0, 0))],
        out_specs=[pl.BlockSpec((tm, KV_LATENT), lambda i: (i, 0)),
                   pl.BlockSpec((tm, IDX_DIM), lambda i: (i, 0)),
                   pl.BlockSpec((tm, LANE), lambda i: (i, 0))],
        out_shape=[jax.ShapeDtypeStruct((t, KV_LATENT), BF16),
                   jax.ShapeDtypeStruct((t, IDX_DIM), BF16),
                   jax.ShapeDtypeStruct((t, LANE), F32)],
        compiler_params=_params("arbitrary"),
        name="small_proj",
    )(h, w_small, kv_g.reshape(1, KV_LATENT))


def _indexer_kernel(q_ref, k_ref, w_ref, bias_ref, key_scr, keyt_scr, ngt_scr, run_scr,
                    *, n_kt, k_sel):
    i = pl.program_id(1)
    q = q_ref[...]
    qs = jnp.concatenate([q[:, h * IDX_DIM:(h + 1) * IDX_DIM] for h in range(N_HEADS)], axis=0)
    w = w_ref[...]
    wcols = [w[:, h:h + 1] for h in range(N_HEADS)]
    row = lax.broadcasted_iota(I32, (Q_BLOCK, KEY_TILE), 0)
    col = lax.broadcasted_iota(I32, (Q_BLOCK, KEY_TILE), 1)
    q_chunk = (i * Q_BLOCK + row) // CHUNK
    n_used = (i * Q_BLOCK) // KEY_TILE + 1

    for j in range(n_kt):
        @pl.when(j < n_used)
        def _():
            k = k_ref[0, j * KEY_TILE:(j + 1) * KEY_TILE, :]
            logits = lax.dot_general(qs, k, (((1,), (1,)), ((), ())),
                                     preferred_element_type=F32)
            sc = wcols[0] * jnp.maximum(logits[0:Q_BLOCK], 0.0)
            for h in range(1, N_HEADS):
                sc = sc + wcols[h] * jnp.maximum(logits[h * Q_BLOCK:(h + 1) * Q_BLOCK], 0.0)
            bits = lax.bitcast_convert_type(sc + 0.0, I32)
            key = bits ^ ((bits >> 31) & 0x7FFFFFFF)
            allowed = (j * KEY_TILE + col) // CHUNK <= q_chunk
            key = jnp.where(allowed, key, INT_MIN)
            key_scr[j] = key
            keyt_scr[j] = key.T

    def count_tile(hit):
        ones = jnp.where(hit, 1.0, 0.0)
        return jnp.sum(ones.reshape(KEY_TILE // COUNT_ROWS, COUNT_ROWS, Q_BLOCK), axis=0)

    def bisect(it, t_u):
        cand_u = t_u | jnp.left_shift(jnp.int32(1), 31 - it)
        cand_s = cand_u ^ INT_MIN

        def count(j, cnt):
            return cnt + count_tile(keyt_scr[j] >= cand_s)

        cnt = lax.fori_loop(0, n_used, count, jnp.zeros((COUNT_ROWS, Q_BLOCK), F32))
        tot = jnp.sum(cnt, axis=0, keepdims=True)
        return jnp.where(tot >= float(k_sel), cand_u, t_u)

    t_u = lax.fori_loop(0, 32, bisect, jnp.zeros((1, Q_BLOCK), I32))
    t_s = jnp.maximum(t_u ^ INT_MIN, INT_MIN + 1)
    t_col = jnp.broadcast_to(t_s, (Q_BLOCK, Q_BLOCK)).T
    t_tile = jnp.concatenate([t_col] * (KEY_TILE // Q_BLOCK), axis=1)

    def count_ge(j, cnt):
        return cnt + count_tile(keyt_scr[j] >= t_s)

    n_ge = jnp.sum(lax.fori_loop(0, n_used, count_ge, jnp.zeros((COUNT_ROWS, Q_BLOCK), F32)),
                   axis=0, keepdims=True)
    has_ties = jnp.max(n_ge) > float(k_sel)

    def widen(x):
        return jnp.concatenate([x] * (KEY_TILE // LANE), axis=1)

    @pl.when(jnp.logical_not(has_ties))
    def _():
        for j in range(n_kt):
            @pl.when(j < n_used)
            def _():
                bias_ref[0, 0, j] = jnp.where(key_scr[j] >= t_tile, 0.0,
                                              NEG_BIAS).astype(bias_ref.dtype)

    @pl.when(has_ties)
    def _():
        ngt_scr[...] = jnp.zeros(ngt_scr.shape, F32)
        run_scr[...] = jnp.zeros(run_scr.shape, F32)
        for j in range(n_kt):
            @pl.when(j < n_used)
            def _():
                ngt_scr[...] += jnp.sum(jnp.where(key_scr[j] > t_tile, 1.0, 0.0),
                                        axis=-1, keepdims=True)
        a = lax.broadcasted_iota(I32, (KEY_TILE, KEY_TILE), 0)
        b = lax.broadcasted_iota(I32, (KEY_TILE, KEY_TILE), 1)
        before = jnp.where(a < b, 1.0, 0.0).astype(BF16)
        room = widen(float(k_sel) - ngt_scr[...])
        for j in range(n_kt):
            @pl.when(j < n_used)
            def _():
                key = key_scr[j]
                tie = jnp.where(key == t_tile, 1.0, 0.0)
                ahead = jnp.dot(tie.astype(BF16), before, preferred_element_type=F32) \
                    + widen(run_scr[...])
                keep_tie = jnp.where(ahead < room, tie, 0.0)
                keep = jnp.where(key > t_tile, 1.0, keep_tie)
                bias_ref[0, 0, j] = jnp.where(keep > 0.5, 0.0, NEG_BIAS).astype(bias_ref.dtype)
                run_scr[...] += jnp.sum(tie, axis=-1, keepdims=True)

    for j in range(n_kt):
        @pl.when(j >= n_used)
        def _():
            bias_ref[0, 0, j] = jnp.full((Q_BLOCK, KEY_TILE), NEG_BIAS, bias_ref.dtype)


def _indexer(q_idx, k_idx3, w_idx, bsz, seq):
    n_q = seq // Q_BLOCK
    n_kt = seq // KEY_TILE
    k_sel = min(TOPK_KEYS_MAX, seq // 4)
    return pl.pallas_call(
        functools.partial(_indexer_kernel, n_kt=n_kt, k_sel=k_sel),
        grid=(bsz, n_q),
        in_specs=[pl.BlockSpec((Q_BLOCK, N_HEADS * IDX_DIM), lambda b, i: (b * n_q + i, 0)),
                  pl.BlockSpec((1, seq, IDX_DIM), lambda b, i: (b, 0, 0)),
                  pl.BlockSpec((Q_BLOCK, LANE), lambda b, i: (b * n_q + i, 0))],
        out_specs=pl.BlockSpec((1, 1, n_kt, Q_BLOCK, KEY_TILE), lambda b, i: (b, i, 0, 0, 0)),
        out_shape=jax.ShapeDtypeStruct((bsz, n_q, n_kt, Q_BLOCK, KEY_TILE), BF16),
        scratch_shapes=[pltpu.VMEM((n_kt, Q_BLOCK, KEY_TILE), I32),
                        pltpu.VMEM((n_kt, KEY_TILE, Q_BLOCK), I32),
                        pltpu.VMEM((Q_BLOCK, LANE), F32),
                        pltpu.VMEM((Q_BLOCK, LANE), F32)],
        compiler_params=_params("arbitrary", "arbitrary"),
        name="indexer",
    )(q_idx, k_idx3, w_idx)


def _attn_kernel(qa_ref, c_ref, bias_ref, wuk_ref, wuv_ref, o_ref,
                 s_scr, q_scr, m_scr, l_scr, acc_scr):
    i = pl.program_id(1)
    n_used = (i * Q_BLOCK) // KEY_TILE + 1
    rows = N_HEADS * Q_BLOCK

    qa = qa_ref[...]
    for h in range(N_HEADS):
        ql = jnp.dot(qa[:, h * HEAD_DIM:(h + 1) * HEAD_DIM], wuk_ref[h],
                     preferred_element_type=F32)
        q_scr[h * Q_BLOCK:(h + 1) * Q_BLOCK, :] = (ql * (ATTN_SCALE * LOG2_E)).astype(BF16)

    m_scr[...] = jnp.full((rows, LANE), -jnp.inf, F32)

    def scores(j, carry):
        off = pl.multiple_of(j * KEY_TILE, KEY_TILE)
        k = c_ref[0, pl.ds(off, KEY_TILE), :]
        s = lax.dot_general(q_scr[...], k, (((1,), (1,)), ((), ())),
                            preferred_element_type=F32)
        b = bias_ref[0, 0, j].astype(F32)
        for h in range(N_HEADS):
            sl = slice(h * Q_BLOCK, (h + 1) * Q_BLOCK)
            sb = s[sl] + b
            s_scr[j, sl, :] = sb
            m_scr[sl, :] = jnp.maximum(m_scr[sl, :], jnp.maximum(sb[:, :LANE], sb[:, LANE:]))
        return carry

    lax.fori_loop(0, n_used, scores, 0)

    m = jnp.max(m_scr[...], axis=-1, keepdims=True)
    l_scr[...] = jnp.zeros((rows, LANE), F32)
    acc_scr[...] = jnp.zeros((rows, KV_LATENT), F32)

    def values(j, carry):
        off = pl.multiple_of(j * KEY_TILE, KEY_TILE)
        p = jnp.exp2(s_scr[j] - m)
        l_scr[...] += p[:, :LANE] + p[:, LANE:]
        acc_scr[...] += jnp.dot(p.astype(BF16), c_ref[0, pl.ds(off, KEY_TILE), :],
                                preferred_element_type=F32)
        return carry

    lax.fori_loop(0, n_used, values, 0)

    inv_l = 1.0 / jnp.sum(l_scr[...], axis=-1, keepdims=True)
    o = (acc_scr[...] * inv_l).astype(BF16)
    for h in range(N_HEADS):
        o_ref[:, h * HEAD_DIM:(h + 1) * HEAD_DIM] = jnp.dot(
            o[h * Q_BLOCK:(h + 1) * Q_BLOCK], wuv_ref[h],
            preferred_element_type=F32).astype(o_ref.dtype)


def _attention(q_a, c_kv3, bias, w_uk, w_uv, bsz, seq):
    n_q = seq // Q_BLOCK
    n_kt = seq // KEY_TILE
    rows = N_HEADS * Q_BLOCK
    t = bsz * seq
    return pl.pallas_call(
        _attn_kernel,
        grid=(bsz, n_q),
        in_specs=[pl.BlockSpec((Q_BLOCK, N_HEADS * HEAD_DIM), lambda b, i: (b * n_q + i, 0)),
                  pl.BlockSpec((1, seq, KV_LATENT), lambda b, i: (b, 0, 0)),
                  pl.BlockSpec((1, 1, n_kt, Q_BLOCK, KEY_TILE), lambda b, i: (b, i, 0, 0, 0)),
                  pl.BlockSpec((N_HEADS, HEAD_DIM, KV_LATENT), lambda b, i: (0, 0, 0)),
                  pl.BlockSpec((N_HEADS, KV_LATENT, HEAD_DIM), lambda b, i: (0, 0, 0))],
        out_specs=pl.BlockSpec((Q_BLOCK, N_HEADS * HEAD_DIM), lambda b, i: (b * n_q + i, 0)),
        out_shape=jax.ShapeDtypeStruct((t, N_HEADS * HEAD_DIM), BF16),
        scratch_shapes=[pltpu.VMEM((n_kt, rows, KEY_TILE), F32),
                        pltpu.VMEM((rows, KV_LATENT), BF16),
                        pltpu.VMEM((rows, LANE), F32),
                        pltpu.VMEM((rows, LANE), F32),
                        pltpu.VMEM((rows, KV_LATENT), F32)],
        compiler_params=_params("arbitrary", "arbitrary"),
        name="attention",
    )(q_a, c_kv3, bias, w_uk, w_uv)


def _spatial_kernel(z_ref, g_ref, ws_ref, bs_ref, o_ref, *, width):
    z = z_ref[...].astype(F32)
    z = 0.5 * z * (1.0 + lax.erf(z * (2.0 ** -0.5)))
    u = z[:, :width]
    v = z[:, width:]
    mu = jnp.mean(v, axis=-1, keepdims=True)
    vc = v - mu
    var = jnp.mean(vc * vc, axis=-1, keepdims=True)
    vn = (vc * lax.rsqrt(var + EPS) * g_ref[...]).astype(BF16)
    r = lax.broadcasted_iota(I32, (SG_CHUNK, SG_CHUNK), 0)
    c = lax.broadcasted_iota(I32, (SG_CHUNK, SG_CHUNK), 1)
    gd = width // N_GROUPS_B
    bs = bs_ref[...]
    for g in range(N_GROUPS_B):
        wg = jnp.where(r >= c, ws_ref[g], 0.0).astype(BF16)
        s = jnp.dot(wg, vn[:, g * gd:(g + 1) * gd], preferred_element_type=F32) + bs[:, g:g + 1]
        o_ref[:, g * gd:(g + 1) * gd] = (u[:, g * gd:(g + 1) * gd] * s).astype(o_ref.dtype)


def _spatial_gating(z, sg_g, w_spatial, b_spatial_t):
    t, w2 = z.shape
    width = w2 // 2
    return pl.pallas_call(
        functools.partial(_spatial_kernel, width=width),
        grid=(t // SG_CHUNK,),
        in_specs=[pl.BlockSpec((SG_CHUNK, w2), lambda i: (i, 0)),
                  pl.BlockSpec((1, width), lambda i: (0, 0)),
                  pl.BlockSpec((N_GROUPS_B, SG_CHUNK, SG_CHUNK), lambda i: (0, 0, 0)),
                  pl.BlockSpec((SG_CHUNK, LANE), lambda i: (0, 0))],
        out_specs=pl.BlockSpec((SG_CHUNK, width), lambda i: (i, 0)),
        out_shape=jax.ShapeDtypeStruct((t, width), BF16),
        compiler_params=_params("arbitrary"),
        name="spatial_gating",
    )(z, sg_g.reshape(1, width), w_spatial, b_spatial_t)


def _mix_kernel(a_ref, b_ref, wa_ref, wb_ref, ga_ref, gb_ref, o_ref):
    ya = jnp.dot(a_ref[...], wa_ref[...], preferred_element_type=F32)
    yb = jnp.dot(b_ref[...], wb_ref[...], preferred_element_type=F32)
    o_ref[...] = (ga_ref[...].astype(F32) * ya + gb_ref[...].astype(F32) * yb).astype(o_ref.dtype)


def _mix(o_a, sg, w_a, w_b, gates):
    t, k = o_a.shape
    d = w_a.shape[1]
    tm, tn = 1024, 1024
    nb = d // tn
    return pl.pallas_call(
        _mix_kernel,
        grid=(nb, t // tm),
        in_specs=[pl.BlockSpec((tm, k), lambda j, i: (i, 0)),
                  pl.BlockSpec((tm, k), lambda j, i: (i, 0)),
                  pl.BlockSpec((k, tn), lambda j, i: (0, j)),
                  pl.BlockSpec((k, tn), lambda j, i: (0, j)),
                  pl.BlockSpec((tm, tn), lambda j, i: (i, j)),
                  pl.BlockSpec((tm, tn), lambda j, i: (i, nb + j))],
        out_specs=pl.BlockSpec((tm, tn), lambda j, i: (i, j)),
        out_shape=jax.ShapeDtypeStruct((t, d), BF16),
        compiler_params=_params("arbitrary", "arbitrary"),
        name="branch_mix",
    )(o_a, sg, w_a, w_b, gates, gates)


def _out_router_kernel(a_ref, w_ref, x_ref, mod_ref, g_ref, wr_ref, br_ref,
                       x1_ref, h2_ref, lg_ref):
    mix = jnp.dot(a_ref[...], w_ref[...], preferred_element_type=F32)
    x1 = x_ref[...] + mod_ref[0, 2:3, :] * mix
    x1_ref[...] = x1
    y = x1 * lax.rsqrt(jnp.mean(x1 * x1, axis=-1, keepdims=True) + EPS) * g_ref[...]
    h2 = y * (1.0 + mod_ref[0, 4:5, :]) + mod_ref[0, 3:4, :]
    h2_ref[...] = h2
    h_hi = h2.astype(BF16)
    h_lo = (h2 - h_hi.astype(F32)).astype(BF16)
    wr = wr_ref[...]
    w_hi = wr.astype(BF16)
    w_lo = (wr - w_hi.astype(F32)).astype(BF16)
    lg = jnp.dot(h_hi, w_hi, preferred_element_type=F32)
    lg = lg + jnp.dot(h_lo, w_hi, preferred_element_type=F32)
    lg = lg + jnp.dot(h_hi, w_lo, preferred_element_type=F32)
    lg_ref[...] = lg + br_ref[...]


def _out_router(mixpre, w_out, x2, mod6, norm2_g, w_router_p, b_router_p, seq):
    t, d = x2.shape
    tm = 256
    return pl.pallas_call(
        _out_router_kernel,
        grid=(t // tm,),
        in_specs=[pl.BlockSpec((tm, d), lambda i: (i, 0)),
                  pl.BlockSpec((d, d), lambda i: (0, 0)),
                  pl.BlockSpec((tm, d), lambda i: (i, 0)),
                  pl.BlockSpec((1, 6, d), lambda i: ((i * tm) // seq, 0, 0)),
                  pl.BlockSpec((1, d), lambda i: (0, 0)),
                  pl.BlockSpec((d, LANE), lambda i: (0, 0)),
                  pl.BlockSpec((1, LANE), lambda i: (0, 0))],
        out_specs=[pl.BlockSpec((tm, d), lambda i: (i, 0)),
                   pl.BlockSpec((tm, d), lambda i: (i, 0)),
                   pl.BlockSpec((tm, LANE), lambda i: (i, 0))],
        out_shape=[jax.ShapeDtypeStruct((t, d), F32),
                   jax.ShapeDtypeStruct((t, d), F32),
                   jax.ShapeDtypeStruct((t, LANE), F32)],
        compiler_params=_params("arbitrary"),
        name="out_proj_router",
    )(mixpre, w_out, x2, mod6, norm2_g.reshape(1, d), w_router_p, b_router_p)


DISPATCH_TOKENS = 512


def _pack_bf16_pairs(x):
    half = x.shape[1] // 2
    hi = lax.bitcast_convert_type(x[:, :half].astype(BF16).astype(F32), I32)
    lo = lax.bitcast_convert_type(x[:, half:].astype(BF16).astype(F32), I32)
    return hi | lax.shift_right_logical(lo, 16)


def _unpack_bf16_pairs(p):
    hi = lax.bitcast_convert_type(p & jnp.int32(-65536), F32).astype(BF16)
    lo = lax.bitcast_convert_type(lax.shift_left(p, 16), F32).astype(BF16)
    return hi, lo


def _dispatch_kernel(pos_ref, pad_ref, h_ref, xs_hbm, pk_scr, zero_scr, sem, *, n_alloc_blocks):
    i = pl.program_id(0)
    n_tiles = pl.num_programs(0)
    packed = _pack_bf16_pairs(h_ref[...])
    for c in range(pk_scr.shape[1]):
        pk_scr[:, c, :] = packed[:, c * LANE:(c + 1) * LANE]
    zero_scr[...] = jnp.zeros(zero_scr.shape, zero_scr.dtype)

    def row_copy(t, k):
        return pltpu.make_async_copy(
            pk_scr.at[t], xs_hbm.at[pos_ref[0, 0, t * TOP_K + k]], sem)

    def pad_copy(r):
        return pltpu.make_async_copy(zero_scr.at[0], xs_hbm.at[r], sem)

    def tail_copy(b):
        off = pl.multiple_of(b * MOE_ROWS, MOE_ROWS)
        return pltpu.make_async_copy(zero_scr, xs_hbm.at[pl.ds(off, MOE_ROWS)], sem)

    e_lo = (i * N_EXPERTS) // n_tiles
    e_hi = ((i + 1) * N_EXPERTS) // n_tiles
    tail_lo = jnp.where(i == 0, pad_ref[2, 0], n_alloc_blocks)

    def for_rows(fn):
        def per_token(t, carry):
            for k in range(TOP_K):
                fn(row_copy(t, k), k)
            return carry

        def per_expert(e, carry):
            def per_pad(r, c):
                fn(pad_copy(r), 0)
                return c
            return lax.fori_loop(pad_ref[0, e], pad_ref[1, e], per_pad, carry)

        def per_tail(b, carry):
            fn(tail_copy(b), 0)
            return carry

        lax.fori_loop(0, DISPATCH_TOKENS, per_token, 0)
        lax.fori_loop(e_lo, e_hi, per_expert, 0)
        lax.fori_loop(tail_lo, n_alloc_blocks, per_tail, 0)

    for_rows(lambda cp, k: cp.start(priority=k % 2))
    for_rows(lambda cp, k: cp.wait())


def _dispatch(pos, pad_rng, h2, n_alloc_rows):
    t, d = h2.shape
    n_tiles = t // DISPATCH_TOKENS
    pos3 = pos.reshape(n_tiles, 1, DISPATCH_TOKENS * TOP_K)
    return pl.pallas_call(
        functools.partial(_dispatch_kernel, n_alloc_blocks=n_alloc_rows // MOE_ROWS),
        grid=(n_tiles,),
        in_specs=[pl.BlockSpec((1, 1, DISPATCH_TOKENS * TOP_K), lambda i: (i, 0, 0),
                               memory_space=pltpu.SMEM),
                  pl.BlockSpec(memory_space=pltpu.SMEM),
                  pl.BlockSpec((DISPATCH_TOKENS, d), lambda i: (i, 0))],
        out_specs=pl.BlockSpec(memory_space=pl.ANY),
        out_shape=jax.ShapeDtypeStruct((n_alloc_rows, d // (2 * LANE), LANE), I32),
        scratch_shapes=[pltpu.VMEM((DISPATCH_TOKENS, d // (2 * LANE), LANE), I32),
                        pltpu.VMEM((MOE_ROWS, d // (2 * LANE), LANE), I32),
                        pltpu.SemaphoreType.DMA(())],
        compiler_params=_params("arbitrary"),
        name="moe_dispatch",
    )(pos3, pad_rng, h2)


def _ffn_kernel(ie_ref, in_ref, ix_ref, io_ref, no_ref, xs_hbm, wgu_hbm, wdn_hbm, bg_ref, bl_ref,
                bd_ref, o_hbm, x_scr, xw_scr, acc_scr, wg_buf, wl_buf, wd_buf, sem, sem_x, sem_w):
    i = pl.program_id(0)
    f = pl.program_id(1)
    n_items = pl.num_programs(0)
    n_f = pl.num_programs(1)
    n_sub = in_ref[i]
    half = x_scr.shape[1] // 2
    r = MOE_ROWS
    win = xw_scr.shape[1]
    tf = wg_buf.shape[2]
    d_ff = wdn_hbm.shape[1]
    o_ref = acc_scr.at[i % 2]
    step = i * n_f + f
    ahead = WEIGHT_SLOTS - 1

    def for_weights(g, fn):
        item = g // n_f
        ft = g - item * n_f
        e = ie_ref[item]
        slot = g % WEIGHT_SLOTS
        col = pl.multiple_of(ft * tf, tf)
        fn(pltpu.make_async_copy(wgu_hbm.at[e, :, pl.ds(col, tf)], wg_buf.at[slot],
                                 sem_w.at[slot]))
        fn(pltpu.make_async_copy(wgu_hbm.at[e, :, pl.ds(d_ff + col, tf)], wl_buf.at[slot],
                                 sem_w.at[slot]))
        fn(pltpu.make_async_copy(wdn_hbm.at[e, pl.ds(col, tf), :], wd_buf.at[slot],
                                 sem_w.at[slot]))

    def fetch_if_used(g):
        @pl.when(g < n_items * n_f)
        def _():
            @pl.when(in_ref[g // n_f] > 0)
            def _():
                for_weights(g, lambda cp: cp.start())

    @pl.when(step == 0)
    def _():
        for g in range(ahead):
            fetch_if_used(g)

    fetch_if_used(step + ahead)

    @pl.when(n_sub > 0)
    def _():
        for_weights(step, lambda cp: cp.wait())

    w_slot = step % WEIGHT_SLOTS

    def for_window(item, fn):
        off = pl.multiple_of(ix_ref[item] * r, r)
        for c in range(half // LANE):
            fn(pltpu.make_async_copy(xs_hbm.at[pl.ds(off, win), c, :],
                                     xw_scr.at[item % 2, :, pl.ds(c * LANE, LANE)],
                                     sem_x.at[item % 2]))

    def out_copy(item, s):
        off = pl.multiple_of(io_ref[item] * r, r)
        return pltpu.make_async_copy(acc_scr.at[item % 2, pl.ds(0, s * r), pl.ds(0, half)],
                                     o_hbm.at[pl.ds(off, s * r), :], sem.at[item % 2])

    def for_out(item, fn):
        for s in range(1, ITEM_SUB + 1):
            @pl.when(no_ref[item] == s)
            def _():
                fn(out_copy(item, s))

    @pl.when(f == 0)
    def _():
        @pl.when(i == 0)
        def _():
            for_window(0, lambda cp: cp.start())

        @pl.when(i + 1 < n_items)
        def _():
            for_window(i + 1, lambda cp: cp.start())

        for_window(i, lambda cp: cp.wait())

        @pl.when(i >= 2)
        def _():
            for_out(i - 2, lambda cp: cp.wait())
        o_ref[...] = jnp.broadcast_to(bd_ref[0], o_ref.shape)

    def unpack(rows):
        hi, lo = _unpack_bf16_pairs(xw_scr[i % 2, :rows, :])
        x_scr[:rows, :half] = hi
        x_scr[:rows, half:] = lo

    def ffn(xb):
        g = jnp.dot(xb, wg_buf[w_slot].astype(BF16), preferred_element_type=F32) + bg_ref[0]
        lin = jnp.dot(xb, wl_buf[w_slot].astype(BF16), preferred_element_type=F32) + bl_ref[0]
        g = jnp.minimum(g, SWIGLU_LIMIT)
        lin = jnp.clip(lin, -SWIGLU_LIMIT, SWIGLU_LIMIT)
        act = g * jax.nn.sigmoid(SWIGLU_ALPHA * g) * (lin + 1.0)
        return jnp.dot(act.astype(BF16), wd_buf[w_slot].astype(BF16), preferred_element_type=F32)

    for s in range(1, ITEM_SUB + 1):
        rows = s * r

        @pl.when(jnp.logical_and(n_sub == s, f == 0))
        def _():
            unpack(rows)

        @pl.when(n_sub == s)
        def _():
            o_ref[:rows, :] += ffn(x_scr[:rows, :])

    @pl.when(f == n_f - 1)
    def _():
        o_ref[:, :half] = lax.bitcast_convert_type(_pack_bf16_pairs(o_ref[...]), F32)
        for_out(i, lambda cp: cp.start())

        @pl.when(i == n_items - 1)
        def _():
            @pl.when(i >= 1)
            def _():
                for_out(i - 1, lambda cp: cp.wait())
            for_out(i, lambda cp: cp.wait())


def _expert_ffn(items, xs, w_gate_up, b_gate_up3, w_down, b_down3):
    p = xs.shape[0]
    dh = xs.shape[1] * xs.shape[2]
    d = 2 * dh
    n_e, _, f2 = w_gate_up.shape
    d_ff = f2 // 2
    tf = FFN_TILE
    nf = d_ff // tf
    n_items = items[0].shape[0]
    win = ITEM_SUB * MOE_ROWS

    def ft(f, ni, i):
        return jnp.where(ni[i] > 0, f, nf - 1)

    grid_spec = pltpu.PrefetchScalarGridSpec(
        num_scalar_prefetch=5,
        grid=(n_items, nf),
        in_specs=[pl.BlockSpec(memory_space=pl.ANY),
                  pl.BlockSpec(memory_space=pl.ANY),
                  pl.BlockSpec(memory_space=pl.ANY),
                  pl.BlockSpec((1, 1, tf), lambda i, f, ie, ni, ix, io, no: (ie[i], 0, ft(f, ni, i))),
                  pl.BlockSpec((1, 1, tf),
                               lambda i, f, ie, ni, ix, io, no: (ie[i], 0, nf + ft(f, ni, i))),
                  pl.BlockSpec((1, 1, d), lambda i, f, ie, ni, ix, io, no: (ie[i], 0, 0))],
        out_specs=pl.BlockSpec(memory_space=pl.ANY),
        scratch_shapes=[pltpu.VMEM((win, d), BF16),
                        pltpu.VMEM((2, win, dh), I32),
                        pltpu.VMEM((2, win, d), F32),
                        pltpu.VMEM((WEIGHT_SLOTS, d, tf), F32),
                        pltpu.VMEM((WEIGHT_SLOTS, d, tf), F32),
                        pltpu.VMEM((WEIGHT_SLOTS, tf, d), F32),
                        pltpu.SemaphoreType.DMA((2,)),
                        pltpu.SemaphoreType.DMA((2,)),
                        pltpu.SemaphoreType.DMA((WEIGHT_SLOTS,))],
    )
    return pl.pallas_call(
        _ffn_kernel,
        grid_spec=grid_spec,
        out_shape=jax.ShapeDtypeStruct((p, dh), F32),
        compiler_params=_params("arbitrary", "arbitrary"),
        name="expert_ffn",
    )(*items, xs, w_gate_up, w_down, b_gate_up3, b_gate_up3, b_down3)


COMBINE_TOKENS = 256


def _combine_kernel(pos_ref, pos_next_ref, ys_hbm, gate_ref, x1_ref, mod_ref, g_ref, o_ref,
                    buf, sem):
    i = pl.program_id(0)
    n = pl.num_programs(0)
    slot = i % 2

    def issue(p_ref, sl):
        def body(t, carry):
            for k in range(TOP_K):
                pltpu.make_async_copy(ys_hbm.at[pl.ds(p_ref[0, 0, t * TOP_K + k], 1), :],
                                      buf.at[sl, k, pl.ds(t, 1), :],
                                      sem.at[sl]).start(priority=k % 2)
            return carry
        lax.fori_loop(0, COMBINE_TOKENS, body, 0)

    def wait_all(sl):
        def body(t, carry):
            for k in range(TOP_K):
                pltpu.make_async_copy(ys_hbm.at[pl.ds(0, 1), :],
                                      buf.at[sl, k, pl.ds(t, 1), :], sem.at[sl]).wait()
            return carry
        lax.fori_loop(0, COMBINE_TOKENS, body, 0)

    @pl.when(i == 0)
    def _():
        issue(pos_ref, 0)

    @pl.when(i + 1 < n)
    def _():
        issue(pos_next_ref, 1 - slot)

    wait_all(slot)

    gate = gate_ref[...]
    d = x1_ref.shape[1]
    half = d // 2
    moe_hi = jnp.zeros((COMBINE_TOKENS, half), F32)
    moe_lo = jnp.zeros((COMBINE_TOKENS, half), F32)
    for k in range(TOP_K):
        bits = lax.bitcast_convert_type(buf[slot, k], I32)
        hi = lax.bitcast_convert_type(bits & jnp.int32(-65536), F32)
        lo = lax.bitcast_convert_type(lax.shift_left(bits, 16), F32)
        moe_hi = moe_hi + gate[:, k:k + 1] * hi
        moe_lo = moe_lo + gate[:, k:k + 1] * lo
    x_hi = x1_ref[:, :half] + mod_ref[0, 5:6, :half] * moe_hi
    x_lo = x1_ref[:, half:] + mod_ref[0, 5:6, half:] * moe_lo
    ss = jnp.sum(x_hi * x_hi, axis=-1, keepdims=True) + jnp.sum(x_lo * x_lo, axis=-1, keepdims=True)
    inv = lax.rsqrt(ss * (1.0 / d) + EPS)
    o_ref[:, :half] = x_hi * inv * g_ref[:, :half]
    o_ref[:, half:] = x_lo * inv * g_ref[:, half:]


def _combine(pos, ys, gates_p, x1, mod6, final_g, seq):
    t, d = x1.shape
    tt = COMBINE_TOKENS
    n_tiles = t // tt
    pos3 = pos.reshape(n_tiles, 1, tt * TOP_K)
    return pl.pallas_call(
        _combine_kernel,
        grid=(n_tiles,),
        in_specs=[pl.BlockSpec((1, 1, tt * TOP_K), lambda i: (i, 0, 0), memory_space=pltpu.SMEM),
                  pl.BlockSpec((1, 1, tt * TOP_K), lambda i: (jnp.minimum(i + 1, n_tiles - 1), 0, 0),
                               memory_space=pltpu.SMEM),
                  pl.BlockSpec(memory_space=pl.ANY),
                  pl.BlockSpec((tt, LANE), lambda i: (i, 0)),
                  pl.BlockSpec((tt, d), lambda i: (i, 0)),
                  pl.BlockSpec((1, 6, d), lambda i: ((i * tt) // seq, 0, 0)),
                  pl.BlockSpec((1, d), lambda i: (0, 0))],
        out_specs=pl.BlockSpec((tt, d), lambda i: (i, 0)),
        out_shape=jax.ShapeDtypeStruct((t, d), F32),
        scratch_shapes=[pltpu.VMEM((2, TOP_K, tt, d // 2), F32),
                        pltpu.SemaphoreType.DMA((2,))],
        compiler_params=_params("arbitrary"),
        name="moe_combine",
    )(pos3, pos3, ys, gates_p, x1, mod6, final_g.reshape(1, d))


ROUTE_TOKENS = 512


def _route_kernel(lg_ref, idx_ref, gate_ref, rank_ref, cnt_ref, carry_scr):
    i = pl.program_id(0)
    tt = lg_ref.shape[0]

    @pl.when(i == 0)
    def _():
        carry_scr[...] = jnp.zeros(carry_scr.shape, F32)

    lane = lax.broadcasted_iota(I32, (tt, LANE), 1)
    v = jnp.where(lane < N_EXPERTS, lg_ref[...], -jnp.inf)
    tops, ids, hits = [], [], []
    for _ in range(TOP_K):
        m = jnp.max(v, axis=-1, keepdims=True)
        idx = jnp.min(jnp.where(v == m, lane, LANE), axis=-1, keepdims=True)
        hit = lane == idx
        v = jnp.where(hit, -jnp.inf, v)
        tops.append(m)
        ids.append(idx)
        hits.append(hit)

    exps = [jnp.exp(m - tops[0]) for m in tops]
    total = exps[0]
    for e in exps[1:]:
        total = total + e

    sel = jnp.where(hits[0], 1.0, 0.0)
    for hit in hits[1:]:
        sel = sel + jnp.where(hit, 1.0, 0.0)
    r = lax.broadcasted_iota(I32, (tt, tt), 0)
    c = lax.broadcasted_iota(I32, (tt, tt), 1)
    earlier = jnp.where(c < r, 1.0, 0.0).astype(BF16)
    rank_all = jnp.dot(earlier, sel.astype(BF16), preferred_element_type=F32) + carry_scr[...]
    carry_scr[...] += jnp.sum(sel, axis=0, keepdims=True)
    cnt_ref[...] = carry_scr[...]

    idx_out = jnp.zeros((tt, LANE), I32)
    gate_out = jnp.zeros((tt, LANE), F32)
    rank_out = jnp.zeros((tt, LANE), F32)
    for k in range(TOP_K):
        rk = jnp.sum(jnp.where(hits[k], rank_all, 0.0), axis=-1, keepdims=True)
        idx_out = jnp.where(lane == k, ids[k], idx_out)
        gate_out = jnp.where(lane == k, exps[k] / total, gate_out)
        rank_out = jnp.where(lane == k, rk, rank_out)
    idx_ref[...] = idx_out
    gate_ref[...] = gate_out
    rank_ref[...] = rank_out.astype(I32)


def _route(logits_p):
    t = logits_p.shape[0]
    tt = ROUTE_TOKENS
    return pl.pallas_call(
        _route_kernel,
        grid=(t // tt,),
        in_specs=[pl.BlockSpec((tt, LANE), lambda i: (i, 0))],
        out_specs=[pl.BlockSpec((tt, LANE), lambda i: (i, 0)),
                   pl.BlockSpec((tt, LANE), lambda i: (i, 0)),
                   pl.BlockSpec((tt, LANE), lambda i: (i, 0)),
                   pl.BlockSpec((1, LANE), lambda i: (0, 0))],
        out_shape=[jax.ShapeDtypeStruct((t, LANE), I32),
                   jax.ShapeDtypeStruct((t, LANE), F32),
                   jax.ShapeDtypeStruct((t, LANE), I32),
                   jax.ShapeDtypeStruct((1, LANE), F32)],
        scratch_shapes=[pltpu.VMEM((1, LANE), F32)],
        compiler_params=_params("arbitrary"),
        name="moe_route",
    )(logits_p)


def _routing(logits_p):
    t = logits_p.shape[0]
    idx_p, gates_p, rank_p, cnt = _route(logits_p)
    top_idx = idx_p[:, :TOP_K]
    counts = cnt[0, :N_EXPERTS].astype(I32)
    padded = ((counts + MOE_ROWS - 1) // MOE_ROWS) * MOE_ROWS
    pad_ends = jnp.cumsum(padded)
    pstarts = pad_ends - padded
    pos = (pstarts[top_idx] + rank_p[:, :TOP_K]).astype(I32)

    win = ITEM_SUB * MOE_ROWS
    n_rows = t * TOP_K + N_EXPERTS * MOE_ROWS
    n_alloc = n_rows + win
    n_items = (n_rows // MOE_ROWS + N_EXPERTS * (ITEM_SUB - 1)) // ITEM_SUB + 1

    tail0 = pad_ends[-1]
    pad_rng = jnp.stack([pstarts + counts, pad_ends,
                         jnp.full((N_EXPERTS,), tail0 // MOE_ROWS)]).astype(I32)

    nb_e = padded // MOE_ROWS
    items_e = (nb_e + ITEM_SUB - 1) // ITEM_SUB
    item_ends = jnp.cumsum(items_e)
    item_starts = item_ends - items_e
    n_used = item_ends[-1]
    q = jnp.arange(n_items, dtype=I32)
    e_q = jnp.clip(jnp.searchsorted(item_ends, q, side="right"), 0, N_EXPERTS - 1)
    local = q - item_starts[e_q]
    used = q < n_used
    item_n = jnp.where(used, jnp.minimum(ITEM_SUB, nb_e[e_q] - ITEM_SUB * local), 0)
    item_x = jnp.where(used, pstarts[e_q] // MOE_ROWS + local * ITEM_SUB, 0)
    n_alloc_blocks = n_alloc // MOE_ROWS
    tail_blk = tail0 // MOE_ROWS + (q - n_used) * ITEM_SUB
    item_o = jnp.where(used, item_x, jnp.minimum(tail_blk, n_alloc_blocks - 1))
    item_no = jnp.where(used, item_n, jnp.clip(n_alloc_blocks - tail_blk, 0, ITEM_SUB))
    e_last = jnp.max(jnp.where(items_e > 0, jnp.arange(N_EXPERTS, dtype=I32), 0))
    item_e = jnp.where(used, e_q, e_last)
    items = tuple(a.astype(I32) for a in (item_e, item_n, item_x, item_o, item_no))
    return pos, gates_p, pad_rng, items, n_alloc


def _layer(x2, c_act_in, bsz, seq, w_mod, b_mod, norm1_g, w_in, kv_norm_g, w_uk, w_uv,
           w_proj_a, sg_norm_g, w_spatial, b_spatial, w_proj_b, w_out, norm2_g,
           w_router, b_router, w_gate_up, b_gate_up, w_down, b_down):
    t, d = x2.shape
    w_qa = N_HEADS * HEAD_DIM
    w_qi = N_HEADS * IDX_DIM
    width_b = sg_norm_g.shape[0]

    mod = _modulation(c_act_in, w_mod, b_mod)
    mod6 = mod.reshape(bsz, 6, d)
    h = _norm_mod(x2, norm1_g, mod6[:, :3], seq)

    o0 = w_qa
    o1 = o0 + KV_LATENT
    o2 = o1 + w_qi
    o3 = o2 + IDX_DIM
    o4 = o3 + N_HEADS
    o5 = o4 + 2 * width_b
    w_small = jnp.concatenate(
        [w_in[:, o0:o1], w_in[:, o2:o3], w_in[:, o3:o4],
         jnp.zeros((d, LANE - N_HEADS), w_in.dtype)], axis=1).astype(BF16)

    q_a = _matmul(h, w_in[:, :o0].astype(BF16), name="proj_q_a")
    q_idx = _matmul(h, w_in[:, o1:o2].astype(BF16), name="proj_q_idx")
    z_b = _matmul(h, w_in[:, o4:o5].astype(BF16), name="proj_z_b")
    gates = _matmul(h, w_in[:, o5:].astype(BF16), act="sigmoid", name="proj_gates")
    c_kv, k_idx, w_idx = _small_proj(h, w_small, kv_norm_g)

    bias = _indexer(q_idx, k_idx.reshape(bsz, seq, IDX_DIM), w_idx, bsz, seq)
    o_a = _attention(q_a, c_kv.reshape(bsz, seq, KV_LATENT), bias,
                     w_uk.astype(BF16), w_uv.astype(BF16), bsz, seq)

    b_sp_t = jnp.pad(jnp.transpose(b_spatial), ((0, 0), (0, LANE - N_GROUPS_B)))
    sg = _spatial_gating(z_b, sg_norm_g, w_spatial, b_sp_t)

    mixpre = _mix(o_a, sg, w_proj_a.astype(BF16), w_proj_b.astype(BF16), gates)

    w_router_p = jnp.pad(w_router, ((0, 0), (0, LANE - N_EXPERTS)))
    b_router_p = jnp.pad(b_router, (0, LANE - N_EXPERTS)).reshape(1, LANE)
    x1, h2, logits_p = _out_router(mixpre, w_out.astype(BF16), x2, mod6, norm2_g,
                                   w_router_p, b_router_p, seq)

    pos, gates_p, pad_rng, items, n_alloc = _routing(logits_p)
    xs = _dispatch(pos.reshape(-1), pad_rng, h2, n_alloc)
    n_e = w_gate_up.shape[0]
    ys = _expert_ffn(items, xs, w_gate_up, b_gate_up.reshape(n_e, 1, -1),
                     w_down, b_down.reshape(n_e, 1, -1))
    return pos, ys, gates_p, x1, mod6


def kernel(x, c, w_mod, b_mod, norm1_g, w_in, kv_norm_g, w_uk, w_uv, w_proj_a, sg_norm_g,
           w_spatial, b_spatial, w_proj_b, w_out, norm2_g, w_router, b_router, w_gate_up,
           b_gate_up, w_down, b_down, final_g):
    bsz, seq, d = x.shape
    depth = w_mod.shape[0]
    assert depth == 1, "single-layer block"
    x2 = x.reshape(bsz * seq, d)
    pos, ys, gates_p, x1, mod6 = _layer(
        x2, c, bsz, seq, w_mod[0], b_mod[0], norm1_g[0], w_in[0], kv_norm_g[0], w_uk[0],
        w_uv[0], w_proj_a[0], sg_norm_g[0], w_spatial[0], b_spatial[0], w_proj_b[0],
        w_out[0], norm2_g[0], w_router[0], b_router[0], w_gate_up[0], b_gate_up[0],
        w_down[0], b_down[0])
    out = _combine(pos.reshape(-1), ys, gates_p, x1, mod6, final_g, seq)
    return out.reshape(bsz, seq, d)
```
